```python
import math
import jax, jax.numpy as jnp
from jax import lax
import numpy as np

D_MODEL = 1024
BATCH = 8
SEQ = 4096
DEPTH = 2

DIFF_HEADS = 4
DIFF_QK_DIM = 32
DIFF_V_DIM = 64
MLA_HEADS = 6
MLA_Q_RANK = 256
MLA_KV_RANK = 128
MLA_NOPE_DIM = 64
MLA_ROPE_DIM = 32
MLA_V_DIM = 64
DIL_HEADS = 6
DIL_HEAD_DIM = 64
DIL_PAIRS = ((128, 1), (512, 4), (2048, 16))
BAND_BLOCK = 64
Q_BLOCK = 128
ROPE_THETA = 10000.0
A_QK_WIDTH = DIFF_HEADS * 2 * DIFF_QK_DIM
A_WIDTH = DIFF_HEADS * DIFF_V_DIM
B_WIDTH = MLA_HEADS * MLA_V_DIM
C_WIDTH = DIL_HEADS * DIL_HEAD_DIM
MIX_WIDTH = A_WIDTH + B_WIDTH + C_WIDTH
IN_WIDTH = 2 * A_QK_WIDTH + A_WIDTH + MLA_Q_RANK + MLA_KV_RANK + MLA_ROPE_DIM + 3 * C_WIDTH
PEER_HEADS = 8
PEER_N_KEYS = 128
PEER_N_EXPERTS = PEER_N_KEYS * PEER_N_KEYS
PEER_QUERY_DIM = 256
PEER_HALF_DIM = PEER_QUERY_DIM // 2
PEER_TOPK = 16
PEER_CHUNK = 128
DEEPNORM_ALPHA = (2 * DEPTH) ** 0.25
DEEPNORM_BETA = (8 * DEPTH) ** -0.25
LN_EPS = 1e-5
NEG_INF = -1e30

kernel_name = 'hybrid_diff_mla_dilated_peer_encoder'


def rms_norm(x, g):
    x32 = x.astype(jnp.float32)
    y = x32 * lax.rsqrt(jnp.mean(x32 * x32, axis=-1, keepdims=True) + LN_EPS)
    return (y * g.astype(jnp.float32)).astype(x.dtype)


def layer_norm(x, g, b):
    x32 = x.astype(jnp.float32)
    xc = x32 - jnp.mean(x32, axis=-1, keepdims=True)
    var = jnp.mean(xc * xc, axis=-1, keepdims=True)
    y = xc * lax.rsqrt(var + LN_EPS) * g.astype(jnp.float32) + b.astype(jnp.float32)
    return y.astype(x.dtype)


def rope(t, pos):
    half = t.shape[-1] // 2
    inv_freq = jnp.exp(-math.log(ROPE_THETA) * jnp.arange(half, dtype=jnp.float32) / half)
    ang = pos.astype(jnp.float32)[:, None] * inv_freq[None, :]
    cos, sin = jnp.cos(ang), jnp.sin(ang)
    t32 = t.astype(jnp.float32)
    t1, t2 = t32[..., :half], t32[..., half:]
    return jnp.concatenate([t1 * cos - t2 * sin, t2 * cos + t1 * sin], axis=-1).astype(t.dtype)


def to_heads(t, n_heads):
    b, s, _ = t.shape
    return t.reshape(b, s, n_heads, -1).transpose(0, 2, 1, 3)


def from_heads(t):
    b, h, s, d = t.shape
    return t.transpose(0, 2, 1, 3).reshape(b, s, h * d)


def dense_attention(q, k, v, scale):
    b, h, s, dk = q.shape
    nb = s // Q_BLOCK
    qb = q.reshape(b, h, nb, Q_BLOCK, dk).transpose(2, 0, 1, 3, 4)

    def block(qi):
        sc = jnp.einsum('bhqd,bhkd->bhqk', qi, k, preferred_element_type=jnp.float32) * scale
        p = jax.nn.softmax(sc, axis=-1)
        return jnp.einsum('bhqk,bhkd->bhqd', p.astype(v.dtype), v)

    o = lax.map(block, qb)
    return o.transpose(1, 2, 0, 3, 4).reshape(b, h, s, v.shape[-1])


def banded_attention(q, k, v, half, scale):
    lead = q.shape[:-2]
    length, dk = q.shape[-2], q.shape[-1]
    dv = v.shape[-1]
    nb = -(-length // BAND_BLOCK)
    lp = nb * BAND_BLOCK
    span = BAND_BLOCK + 2 * half
    pad_q = [(0, 0)] * (q.ndim - 2) + [(0, lp - length), (0, 0)]
    pad_kv = [(0, 0)] * (q.ndim - 2) + [(half, half + lp - length), (0, 0)]
    qb = jnp.pad(q, pad_q).reshape(*lead, nb, BAND_BLOCK, dk)
    starts = jnp.arange(nb) * BAND_BLOCK
    idx = starts[:, None] + jnp.arange(span)[None, :]
    kb = jnp.take(jnp.pad(k, pad_kv), idx, axis=-2)
    vb = jnp.take(jnp.pad(v, pad_kv), idx, axis=-2)
    sc = jnp.einsum('...nqd,...nkd->...nqk', qb, kb, preferred_element_type=jnp.float32) * scale
    t = jnp.arange(BAND_BLOCK)[:, None]
    j = jnp.arange(span)[None, :]
    in_band = jnp.abs(j - half - t) <= half
    key_pos = starts[:, None, None] + j[None] - half
    mask = in_band[None] & (key_pos >= 0) & (key_pos < length)
    sc = jnp.where(mask, sc, NEG_INF)
    lse = jax.nn.logsumexp(sc, axis=-1)
    p = jnp.exp(sc - lse[..., None])
    o = jnp.einsum('...nqk,...nkd->...nqd', p.astype(v.dtype), vb)
    o = o.reshape(*lead, lp, dv)[..., :length, :]
    lse = lse.reshape(*lead, lp)[..., :length]
    return o, lse


def to_strided(t, dil):
    b, h, s, d = t.shape
    return t.reshape(b, h, s // dil, dil, d).swapaxes(2, 3)


def diff_attention(q, k, v, lam, subln_g, lambda_init, pos):
    b, s, _ = q.shape
    q = rope(to_heads(q, 2 * DIFF_HEADS), pos).reshape(b, DIFF_HEADS, 2, s, DIFF_QK_DIM)
    k = rope(to_heads(k, 2 * DIFF_HEADS), pos).reshape(b, DIFF_HEADS, 2, s, DIFF_QK_DIM)
    v = to_heads(v, DIFF_HEADS)
    scale = DIFF_QK_DIM ** -0.5
    o1 = dense_attention(q[:, :, 0], k[:, :, 0], v, scale)
    o2 = dense_attention(q[:, :, 1], k[:, :, 1], v, scale)
    lam32 = lam.astype(jnp.float32)
    lam_full = (jnp.exp(jnp.sum(lam32[0] * lam32[1])) - jnp.exp(jnp.sum(lam32[2] * lam32[3]))
                + lambda_init)
    o = o1 - lam_full.astype(o1.dtype) * o2
    o = rms_norm(o, subln_g) * (1.0 - lambda_init)
    return from_heads(o)


def mla_attention(c_q, c_kv, k_rope, q_norm_g, kv_norm_g, w_uq, w_ukv, pos):
    b, s, _ = c_q.shape
    q = to_heads(rms_norm(c_q, q_norm_g) @ w_uq, MLA_HEADS)
    q = jnp.concatenate([q[..., :MLA_NOPE_DIM], rope(q[..., MLA_NOPE_DIM:], pos)], axis=-1)
    kv = to_heads(rms_norm(c_kv, kv_norm_g) @ w_ukv, MLA_HEADS)
    k_nope, v = kv[..., :MLA_NOPE_DIM], kv[..., MLA_NOPE_DIM:]
    k_pe = rope(k_rope[:, None], pos)
    k = jnp.concatenate([k_nope, jnp.broadcast_to(k_pe, (b, MLA_HEADS, s, MLA_ROPE_DIM))], axis=-1)
    o = dense_attention(q, k, v, (MLA_NOPE_DIM + MLA_ROPE_DIM) ** -0.5)
    return from_heads(o)


def dilated_attention(q, k, v, pos):
    b, s, _ = q.shape
    q = rope(to_heads(q, DIL_HEADS), pos)
    k = rope(to_heads(k, DIL_HEADS), pos)
    v = to_heads(v, DIL_HEADS)
    scale = DIL_HEAD_DIM ** -0.5
    outs, lses = [], []
    for window, dil in DIL_PAIRS:
        half = window // (2 * dil)
        o, lse = banded_attention(to_strided(q, dil), to_strided(k, dil), to_strided(v, dil), half, scale)
        outs.append(o.swapaxes(2, 3).reshape(b, DIL_HEADS, s, DIL_HEAD_DIM))
        lses.append(lse.swapaxes(2, 3).reshape(b, DIL_HEADS, s))
    wts = jax.nn.softmax(jnp.stack(lses), axis=0)
    o = jnp.einsum('pbhs,pbhsd->bhsd', wts.astype(v.dtype), jnp.stack(outs))
    return from_heads(o)


def peer_ffn(x, w_q, sub_keys, u, v):
    b, s, d = x.shape
    xt = x.reshape(-1, PEER_CHUNK, d)
    kk = PEER_TOPK

    def chunk(xc):
        c = xc.shape[0]
        q = (xc @ w_q).reshape(c, PEER_HEADS, 2, PEER_HALF_DIM)
        sc = jnp.einsum('chpd,hpnd->chpn', q, sub_keys, preferred_element_type=jnp.float32)
        top_s, top_i = lax.top_k(sc, kk)
        cand = top_s[:, :, 0, :, None] + top_s[:, :, 1, None, :]
        cand_s, cand_i = lax.top_k(cand.reshape(c, PEER_HEADS, kk * kk), kk)
        i1 = jnp.take_along_axis(top_i[:, :, 0], cand_i // kk, axis=-1)
        i2 = jnp.take_along_axis(top_i[:, :, 1], cand_i % kk, axis=-1)
        experts = i1 * PEER_N_KEYS + i2
        gates = jax.nn.softmax(cand_s, axis=-1)
        ue = jnp.take(u, experts, axis=0)
        ve = jnp.take(v, experts, axis=0)
        act = jax.nn.gelu(jnp.einsum('chkd,cd->chk', ue, xc), approximate=False)
        return jnp.einsum('chk,chkd->cd', gates.astype(xc.dtype) * act, ve)

    return lax.map(chunk, xt).reshape(b, s, d)


def setup_inputs(seed: int = 0) -> dict:
    key = jax.random.key(seed)
    ks = jax.random.split(key, 17)

    def nrm(k, shape, scale):
        return jax.random.normal(k, shape, jnp.float32) * scale

    L = DEPTH
    return {
        'x': nrm(ks[0], (BATCH, SEQ, D_MODEL), 1.0),
        'w_in': nrm(ks[1], (L, D_MODEL, IN_WIDTH), D_MODEL ** -0.5),
        'diff_lambda': nrm(ks[2], (L, 4, DIFF_QK_DIM), 0.1),
        'diff_subln_g': 1.0 + nrm(ks[3], (L, DIFF_V_DIM), 0.02),
        'mla_q_norm_g': 1.0 + nrm(ks[4], (L, MLA_Q_RANK), 0.02),
        'mla_kv_norm_g': 1.0 + nrm(ks[5], (L, MLA_KV_RANK), 0.02),
        'mla_w_uq': nrm(ks[6], (L, MLA_Q_RANK, MLA_HEADS * (MLA_NOPE_DIM + MLA_ROPE_DIM)), MLA_Q_RANK ** -0.5),
        'mla_w_ukv': nrm(ks[7], (L, MLA_KV_RANK, MLA_HEADS * (MLA_NOPE_DIM + MLA_V_DIM)), MLA_KV_RANK ** -0.5),
        'w_o': nrm(ks[8], (L, MIX_WIDTH, D_MODEL), MIX_WIDTH ** -0.5 * DEEPNORM_BETA),
        'ln1_g': 1.0 + nrm(ks[9], (L, D_MODEL), 0.02),
        'ln1_b': nrm(ks[10], (L, D_MODEL), 0.02),
        'peer_w_q': nrm(ks[11], (L, D_MODEL, PEER_HEADS * PEER_QUERY_DIM), D_MODEL ** -0.5),
        'peer_sub_keys': nrm(ks[12], (L, PEER_HEADS, 2, PEER_N_KEYS, PEER_HALF_DIM), PEER_HALF_DIM ** -0.5),
        'peer_u': nrm(ks[13], (L, PEER_N_EXPERTS, D_MODEL), D_MODEL ** -0.5),
        'peer_v': nrm(ks[14], (L, PEER_N_EXPERTS, D_MODEL), DEEPNORM_BETA),
        'ln2_g': 1.0 + nrm(ks[15], (L, D_MODEL), 0.02),
        'ln2_b': nrm(ks[16], (L, D_MODEL), 0.02),
    }


def reference(x, w_in, diff_lambda, diff_subln_g, mla_q_norm_g, mla_kv_norm_g, mla_w_uq, mla_w_ukv,
              w_o, ln1_g, ln1_b, peer_w_q, peer_sub_keys, peer_u, peer_v, ln2_g, ln2_b):
    pos = jnp.arange(x.shape[1], dtype=jnp.int32)
    sizes = (A_QK_WIDTH, A_QK_WIDTH, A_WIDTH, MLA_Q_RANK, MLA_KV_RANK, MLA_ROPE_DIM,
             C_WIDTH, C_WIDTH, C_WIDTH)
    cuts = np.cumsum(sizes)[:-1].tolist()
    for l in range(DEPTH):
        lambda_init = 0.8 - 0.6 * math.exp(-0.3 * (l + 1))
        h = x @ w_in[l]
        a_q, a_k, a_v, b_cq, b_ckv, b_kr, c_q, c_k, c_v = jnp.split(h, cuts, axis=-1)
        out_a = diff_attention(a_q, a_k, a_v, diff_lambda[l], diff_subln_g[l], lambda_init, pos)
        out_b = mla_attention(b_cq, b_ckv, b_kr, mla_q_norm_g[l], mla_kv_norm_g[l],
                              mla_w_uq[l], mla_w_ukv[l], pos)
        out_c = dilated_attention(c_q, c_k, c_v, pos)
        mix = jnp.concatenate([out_a, out_b, out_c], axis=-1) @ w_o[l]
        x = layer_norm(DEEPNORM_ALPHA * x + mix, ln1_g[l], ln1_b[l])
        ffn = peer_ffn(x, peer_w_q[l], peer_sub_keys[l], peer_u[l], peer_v[l])
        x = layer_norm(DEEPNORM_ALPHA * x + ffn, ln2_g[l], ln2_b[l])
    return x
```

```python
import functools
import math

import jax
import jax.numpy as jnp
from jax import lax
from jax.experimental import pallas as pl
from jax.experimental.pallas import tpu as pltpu

BF = jnp.bfloat16
F32 = jnp.float32

LANES = 128
D_MODEL = 1024
DEPTH = 2
DIFF_HEADS = 4
DIFF_QK_DIM = 32
DIFF_V_DIM = 64
MLA_HEADS = 6
MLA_Q_RANK = 256
MLA_KV_RANK = 128
MLA_NOPE_DIM = 64
MLA_ROPE_DIM = 32
MLA_V_DIM = 64
DIL_HEADS = 6
DIL_HEAD_DIM = 64
DIL_PAIRS = ((128, 1), (512, 4), (2048, 16))
ROPE_THETA = 10000.0
PEER_HEADS = 8
PEER_N_KEYS = 128
PEER_N_EXPERTS = PEER_N_KEYS * PEER_N_KEYS
PEER_HALF_DIM = 128
PEER_TOPK = 16
DEEPNORM_ALPHA = (2 * DEPTH) ** 0.25
LN_EPS = 1e-5
NEG_INF = -1e30

VMEM_LIMIT = 56 * 1024 * 1024

TM_PROJ = 512
TQ_DENSE = 256
TQ_BAND = 256
BAND_HALF = 64
TT_PEER = 512
EB_PEER = 1024


def _cparams(sem):
    return pltpu.CompilerParams(dimension_semantics=sem, vmem_limit_bytes=VMEM_LIMIT)


def _dot(a, b):
    return jnp.dot(a, b, preferred_element_type=F32)


def _dot_nt(a, b):
    return lax.dot_general(a, b, (((1,), (1,)), ((), ())), preferred_element_type=F32)


def _pad_heads(w, n_heads, d, offset=0):
    k = w.shape[0]
    w3 = w.reshape(k, n_heads, d)
    out = jnp.zeros((k, n_heads, LANES), w.dtype).at[:, :, offset:offset + d].set(w3)
    return out.reshape(k, n_heads * LANES)


def _rot_cols(w, n_heads, d):
    k = w.shape[0]
    w3 = w.reshape(k, n_heads, d)
    half = d // 2
    return jnp.concatenate([-w3[..., half:], w3[..., :half]], axis=-1).reshape(k, n_heads * d)


def _rope_tables(seq, d, offset, scale, fill):
    half = d // 2
    inv_freq = jnp.exp(-math.log(ROPE_THETA) * jnp.arange(half, dtype=F32) / half)
    ang = jnp.arange(seq, dtype=jnp.int32).astype(F32)[:, None] * inv_freq[None, :]
    cos, sin = jnp.cos(ang), jnp.sin(ang)
    cos_t = jnp.full((seq, LANES), fill, F32).at[:, offset:offset + d].set(jnp.concatenate([cos, cos], -1))
    sin_t = jnp.zeros((seq, LANES), F32).at[:, offset:offset + d].set(jnp.concatenate([sin, sin], -1))
    return cos_t * scale, sin_t * scale


def _rope_store(o_ref, h, hr, cos, sin):
    for s in range(o_ref.shape[1] // LANES):
        sl = slice(s * LANES, (s + 1) * LANES)
        o_ref[:, sl] = (h[:, sl] * cos + hr[:, sl] * sin).astype(o_ref.dtype)


def _proj_qkv_kernel(x_ref, wq_ref, wqr_ref, wk_ref, wkr_ref, wv_ref, cq_ref, sq_ref, ck_ref, sk_ref,
                     q_ref, k_ref, v_ref):
    xb = x_ref[...].astype(BF)
    _rope_store(q_ref, _dot(xb, wq_ref[...]), _dot(xb, wqr_ref[...]), cq_ref[...], sq_ref[...])
    _rope_store(k_ref, _dot(xb, wk_ref[...]), _dot(xb, wkr_ref[...]), ck_ref[...], sk_ref[...])
    v_ref[...] = _dot(xb, wv_ref[...]).astype(v_ref.dtype)


def _proj_qkv(x2d, seq, wq, wqr, wk, wkr, wv, cq, sq, ck, sk):
    t = x2d.shape[0]
    tm = TM_PROJ
    nq, nv = wq.shape[1], wv.shape[1]
    pos_blocks = seq // tm
    full = lambda a: pl.BlockSpec(a.shape, lambda i: (0, 0))
    tab = pl.BlockSpec((tm, LANES), lambda i: (i % pos_blocks, 0))
    return pl.pallas_call(
        _proj_qkv_kernel,
        grid=(t // tm,),
        in_specs=[pl.BlockSpec((tm, D_MODEL), lambda i: (i, 0)),
                  full(wq), full(wqr), full(wk), full(wkr), full(wv), tab, tab, tab, tab],
        out_specs=[pl.BlockSpec((tm, nq), lambda i: (i, 0)),
                   pl.BlockSpec((tm, nq), lambda i: (i, 0)),
                   pl.BlockSpec((tm, nv), lambda i: (i, 0))],
        out_shape=[jax.ShapeDtypeStruct((t, nq), BF), jax.ShapeDtypeStruct((t, nq), BF),
                   jax.ShapeDtypeStruct((t, nv), BF)],
        compiler_params=_cparams(("parallel",)),
        name="proj_qkv",
    )(x2d, wq, wqr, wk, wkr, wv, cq, sq, ck, sk)


def _rms(x, g, n):
    return x * lax.rsqrt(jnp.sum(x * x, axis=-1, keepdims=True) / n + LN_EPS) * g


def _proj_mla_kernel(x_ref, wcq_ref, wckv_ref, wkr_ref, wkrr_ref, gq_ref, gkv_ref,
                     wuq_ref, wuqr_ref, wuk_ref, wuv_ref, cq_ref, sq_ref, ck_ref, sk_ref,
                     q_ref, k_ref, v_ref):
    xb = x_ref[...].astype(BF)
    nq = _rms(_dot(xb, wcq_ref[...]), gq_ref[...], MLA_Q_RANK).astype(BF)
    nkv = _rms(_dot(xb, wckv_ref[...]), gkv_ref[...], MLA_KV_RANK).astype(BF)
    _rope_store(q_ref, _dot(nq, wuq_ref[...]), _dot(nq, wuqr_ref[...]), cq_ref[...], sq_ref[...])
    kpe = _dot(xb, wkr_ref[...]) * ck_ref[...] + _dot(xb, wkrr_ref[...]) * sk_ref[...]
    kn = _dot(nkv, wuk_ref[...])
    for s in range(MLA_HEADS):
        sl = slice(s * LANES, (s + 1) * LANES)
        k_ref[:, sl] = (kn[:, sl] + kpe).astype(k_ref.dtype)
    v_ref[...] = _dot(nkv, wuv_ref[...]).astype(v_ref.dtype)


def _proj_mla(x2d, seq, wcq, wckv, wkr, wkrr, gq, gkv, wuq, wuqr, wuk, wuv, cq, sq, ck, sk):
    t = x2d.shape[0]
    tm = TM_PROJ
    n = MLA_HEADS * LANES
    pos_blocks = seq // tm
    full = lambda a: pl.BlockSpec(a.shape, lambda i: (0, 0))
    tab = pl.BlockSpec((tm, LANES), lambda i: (i % pos_blocks, 0))
    out = pl.BlockSpec((tm, n), lambda i: (i, 0))
    return pl.pallas_call(
        _proj_mla_kernel,
        grid=(t // tm,),
        in_specs=[pl.BlockSpec((tm, D_MODEL), lambda i: (i, 0)),
                  full(wcq), full(wckv), full(wkr), full(wkrr), full(gq), full(gkv),
                  full(wuq), full(wuqr), full(wuk), full(wuv), tab, tab, tab, tab],
        out_specs=[out, out, out],
        out_shape=[jax.ShapeDtypeStruct((t, n), BF)] * 3,
        compiler_params=_cparams(("parallel",)),
        name="proj_mla",
    )(x2d, wcq, wckv, wkr, wkrr, gq, gkv, wuq, wuqr, wuk, wuv, cq, sq, ck, sk)


def _softmax_pv(q, k, v):
    s = _dot_nt(q, k)
    m = jnp.max(s, axis=-1, keepdims=True)
    p = jnp.exp(s - m)
    l = jnp.sum(p, axis=-1, keepdims=True)
    return _dot(p.astype(BF), v) / l


def _diff_attn_kernel(lam_ref, g_ref, q1_ref, q2_ref, k1_ref, k2_ref, v_ref, o_ref, *, lambda_init):
    lam = lam_ref[...]
    lam_full = (jnp.exp(jnp.sum(lam[0:1] * lam[1:2], axis=-1, keepdims=True))
                - jnp.exp(jnp.sum(lam[2:3] * lam[3:4], axis=-1, keepdims=True)) + lambda_init)
    v = v_ref[0]
    o1 = _softmax_pv(q1_ref[0], k1_ref[0], v)
    o2 = _softmax_pv(q2_ref[0], k2_ref[0], v)
    o = o1 - lam_full * o2
    o_ref[0] = (_rms(o, g_ref[...], DIFF_V_DIM) * (1.0 - lambda_init)).astype(o_ref.dtype)


def _diff_attn(q, k, v, lam, g_pad, lambda_init):
    b, s, _ = q.shape
    tq = TQ_DENSE
    qspec = lambda off: pl.BlockSpec((1, tq, LANES), lambda bi, h, i: (bi, i, 2 * h + off))
    kspec = lambda off: pl.BlockSpec((1, s, LANES), lambda bi, h, i: (bi, 0, 2 * h + off))
    return pl.pallas_call(
        functools.partial(_diff_attn_kernel, lambda_init=lambda_init),
        grid=(b, DIFF_HEADS, s // tq),
        in_specs=[pl.BlockSpec(lam.shape, lambda bi, h, i: (0, 0)),
                  pl.BlockSpec(g_pad.shape, lambda bi, h, i: (0, 0)),
                  qspec(0), qspec(1), kspec(0), kspec(1),
                  pl.BlockSpec((1, s, LANES), lambda bi, h, i: (bi, 0, h))],
        out_specs=pl.BlockSpec((1, tq, LANES), lambda bi, h, i: (bi, i, h)),
        out_shape=jax.ShapeDtypeStruct((b, s, DIFF_HEADS * LANES), BF),
        compiler_params=_cparams(("parallel", "parallel", "arbitrary")),
        name="diff_attn",
    )(lam, g_pad, q, q, k, k, v)


def _mla_attn_kernel(q_ref, k_ref, v_ref, o_ref):
    o_ref[0] = _softmax_pv(q_ref[0], k_ref[0], v_ref[0]).astype(o_ref.dtype)


def _mla_attn(q, k, v):
    b, s, _ = q.shape
    tq = TQ_DENSE
    kv = pl.BlockSpec((1, s, LANES), lambda bi, h, i: (bi, 0, h))
    qo = pl.BlockSpec((1, tq, LANES), lambda bi, h, i: (bi, i, h))
    return pl.pallas_call(
        _mla_attn_kernel,
        grid=(b, MLA_HEADS, s // tq),
        in_specs=[qo, kv, kv],
        out_specs=qo,
        out_shape=jax.ShapeDtypeStruct((b, s, MLA_HEADS * LANES), BF),
        compiler_params=_cparams(("parallel", "parallel", "arbitrary")),
        name="mla_attn",
    )(q, k, v)


def _band_attn_kernel(q_ref, kp_ref, kc_ref, kn_ref, vp_ref, vc_ref, vn_ref, o_ref, lse_ref, *, seg_len):
    i = pl.program_id(2)
    tq = q_ref.shape[1]
    blk = LANES
    span = 3 * blk
    kk = jnp.concatenate([kp_ref[0], kc_ref[0], kn_ref[0]], axis=0)
    vv = jnp.concatenate([vp_ref[0], vc_ref[0], vn_ref[0]], axis=0)
    qi = lax.broadcasted_iota(jnp.int32, (blk, span), 0)
    kj = lax.broadcasted_iota(jnp.int32, (blk, span), 1)
    for j in range(tq // blk):
        qpos = i * tq + j * blk + qi
        kpos = i * tq + (j - 1) * blk + kj
        mask = (jnp.abs(kpos - qpos) <= BAND_HALF) & (kpos >= 0) & (kpos < seg_len)
        for h in range(DIL_HEADS):
            cs = slice(h * LANES, (h + 1) * LANES)
            q = q_ref[0, j * blk:(j + 1) * blk, cs]
            k = kk[j * blk:j * blk + span, cs]
            v = vv[j * blk:j * blk + span, cs]
            s = jnp.where(mask, _dot_nt(q, k), NEG_INF)
            m = jnp.max(s, axis=-1, keepdims=True)
            p = jnp.exp(s - m)
            l = jnp.sum(p, axis=-1, keepdims=True)
            o_ref[0, j * blk:(j + 1) * blk, cs] = _dot(p.astype(BF), v) / l
            lse_ref[0, j * blk:(j + 1) * blk, cs] = jnp.broadcast_to(m + jnp.log(l), (blk, LANES))


def _band_attn(q, k, v, dil):
    b, s, w = q.shape
    seg = s // dil
    tq = TQ_BAND
    r = tq // LANES
    nb = seg // LANES
    view = lambda a: a.reshape(b, seg, dil * w)
    cur = pl.BlockSpec((1, tq, w), lambda bi, c, i: (bi, i, c))
    prev = pl.BlockSpec((1, LANES, w), lambda bi, c, i: (bi, jnp.maximum(i * r - 1, 0), c))
    nxt = pl.BlockSpec((1, LANES, w), lambda bi, c, i: (bi, jnp.minimum((i + 1) * r, nb - 1), c))
    o, lse = pl.pallas_call(
        functools.partial(_band_attn_kernel, seg_len=seg),
        grid=(b, dil, seg // tq),
        in_specs=[cur, prev, cur, nxt, prev, cur, nxt],
        out_specs=[cur, cur],
        out_shape=[jax.ShapeDtypeStruct((b, seg, dil * w), F32)] * 2,
        compiler_params=_cparams(("parallel", "parallel", "arbitrary")),
        name="band_attn",
    )(view(q), view(k), view(k), view(k), view(v), view(v), view(v))
    return o.reshape(b, s, w), lse.reshape(b, s, w)


def _layer_norm(y, g, b):
    mu = jnp.mean(y, axis=-1, keepdims=True)
    yc = y - mu
    var = jnp.mean(yc * yc, axis=-1, keepdims=True)
    return yc * lax.rsqrt(var + LN_EPS) * g + b


def _out_proj_kernel(x_ref, oa_ref, ob_ref, o1_ref, o2_ref, o3_ref, l1_ref, l2_ref, l3_ref,
                     wa_ref, wb_ref, wc_ref, g_ref, b_ref, y_ref):
    l1, l2, l3 = l1_ref[...], l2_ref[...], l3_ref[...]
    m = jnp.maximum(jnp.maximum(l1, l2), l3)
    e1, e2, e3 = jnp.exp(l1 - m), jnp.exp(l2 - m), jnp.exp(l3 - m)
    oc = (e1 * o1_ref[...] + e2 * o2_ref[...] + e3 * o3_ref[...]) / (e1 + e2 + e3)
    mix = (_dot(oa_ref[...], wa_ref[...]) + _dot(ob_ref[...], wb_ref[...])
           + _dot(oc.astype(BF), wc_ref[...]))
    y_ref[...] = _layer_norm(DEEPNORM_ALPHA * x_ref[...] + mix, g_ref[...], b_ref[...])


def _out_proj(x2d, oa, ob, o1, o2, o3, l1, l2, l3, wa, wb, wc, g, bb):
    t = x2d.shape[0]
    tm = TM_PROJ
    row = lambda a: pl.BlockSpec((tm, a.shape[1]), lambda i: (i, 0))
    full = lambda a: pl.BlockSpec(a.shape, lambda i: (0, 0))
    return pl.pallas_call(
        _out_proj_kernel,
        grid=(t // tm,),
        in_specs=[row(x2d), row(oa), row(ob), row(o1), row(o2), row(o3), row(l1), row(l2), row(l3),
                  full(wa), full(wb), full(wc), full(g), full(bb)],
        out_specs=row(x2d),
        out_shape=jax.ShapeDtypeStruct((t, D_MODEL), F32),
        compiler_params=_cparams(("parallel",)),
        name="out_proj_ln",
    )(x2d, oa, ob, o1, o2, o3, l1, l2, l3, wa, wb, wc, g, bb)


def _top16(s):
    n, tt = s.shape
    row = lax.broadcasted_iota(jnp.int32, (PEER_TOPK, tt), 0)

    def body(kk, carry):
        work, rank, top = carry
        m = jnp.max(work, axis=0, keepdims=True)
        sel = work == m
        kf = kk.astype(F32)
        rank = jnp.where(sel, kf, rank)
        work = jnp.where(sel, -jnp.inf, work)
        top = jnp.where(row == kk, m, top)
        return work, rank, top

    _, rank, top = lax.fori_loop(
        0, PEER_TOPK, body,
        (s, jnp.full((n, tt), float(PEER_TOPK), F32), jnp.zeros((PEER_TOPK, tt), F32)))
    return top, rank


def _peer_route_kernel(x_ref, wq_ref, keys_ref, e1_ref, cnt_ref, e2_ref, rk2_ref):
    xb = x_ref[...].astype(BF)
    q = _dot(xb, wq_ref[...]).astype(BF)
    tt = q.shape[0]
    nk = PEER_N_KEYS
    for h in range(PEER_HEADS):
        q1 = q[:, (2 * h) * nk:(2 * h + 1) * nk]
        q2 = q[:, (2 * h + 1) * nk:(2 * h + 2) * nk]
        s1 = _dot_nt(keys_ref[2 * h], q1)
        s2 = _dot_nt(keys_ref[2 * h + 1], q2)
        t1, _ = _top16(s1)
        t2, rank2 = _top16(s2)
        cand = jnp.concatenate([t1[k1:k1 + 1] + t2 for k1 in range(PEER_TOPK)], axis=0)
        c0 = t1[0:1] + t2[0:1]

        def body(kk, carry):
            work, z, _ = carry
            m = jnp.max(work, axis=0, keepdims=True)
            work = jnp.where(work == m, -jnp.inf, work)
            return work, z + jnp.exp(m - c0), m

        _, z, tau = lax.fori_loop(0, PEER_TOPK, body,
                                  (cand, jnp.zeros((1, tt), F32), jnp.zeros((1, tt), F32)))
        cnt = jnp.zeros((nk, tt), F32)
        for k2 in range(PEER_TOPK):
            cnt = cnt + jnp.where(s1 + t2[k2:k2 + 1] >= tau, 1.0, 0.0)
        rows = slice(h * nk, (h + 1) * nk)
        e1_ref[rows, :] = jnp.exp(s1 - t1[0:1]) / z
        cnt_ref[rows, :] = cnt
        e2_ref[rows, :] = jnp.exp(s2 - t2[0:1])
        rk2_ref[rows, :] = rank2


def _peer_route(x2d, wq, keys):
    t = x2d.shape[0]
    tt = TT_PEER
    rows = PEER_HEADS * PEER_N_KEYS
    out = pl.BlockSpec((rows, tt), lambda i: (0, i))
    return pl.pallas_call(
        _peer_route_kernel,
        grid=(t // tt,),
        in_specs=[pl.BlockSpec((tt, D_MODEL), lambda i: (i, 0)),
                  pl.BlockSpec(wq.shape, lambda i: (0, 0)),
                  pl.BlockSpec(keys.shape, lambda i: (0, 0, 0))],
        out_specs=[out] * 4,
        out_shape=[jax.ShapeDtypeStruct((rows, t), F32)] * 4,
        compiler_params=_cparams(("parallel",)),
        name="peer_route",
    )(x2d, wq, keys)


def _gelu(x):
    return 0.5 * x * (1.0 + lax.erf(x * (2.0 ** -0.5)))


def _peer_expert_kernel(x_ref, e1_ref, cnt_ref, e2_ref, rk2_ref, u_ref, vt_ref, g_ref, b_ref,
                        y_ref, acc_ref, a_ref):
    j = pl.program_id(1)
    nk = PEER_N_KEYS
    rows_per_step = u_ref.shape[0] // nk

    @pl.when(j == 0)
    def _():
        acc_ref[...] = jnp.zeros_like(acc_ref)

    ht = _dot_nt(u_ref[...], x_ref[...].astype(BF))
    for r in range(rows_per_step):
        a = j * rows_per_step + r
        w = jnp.zeros((nk, ht.shape[1]), F32)
        for h in range(PEER_HEADS):
            e1 = e1_ref[pl.ds(h * nk + a, 1), :]
            cnt = cnt_ref[pl.ds(h * nk + a, 1), :]
            hs = slice(h * nk, (h + 1) * nk)
            w = w + jnp.where(rk2_ref[hs, :] < cnt, e1 * e2_ref[hs, :], 0.0)
        rs = slice(r * nk, (r + 1) * nk)
        a_ref[rs, :] = (w * _gelu(ht[rs, :])).astype(BF)
    acc_ref[...] += _dot(vt_ref[...], a_ref[...])

    @pl.when(j == pl.num_programs(1) - 1)
    def _():
        ffn = acc_ref[...].T
        y_ref[...] = _layer_norm(DEEPNORM_ALPHA * x_ref[...] + ffn, g_ref[...], b_ref[...])


def _peer_expert(x2d, e1, cnt, e2, rk2, u, vt, g, bb):
    t = x2d.shape[0]
    tt, eb = TT_PEER, EB_PEER
    rows = PEER_HEADS * PEER_N_KEYS
    route = pl.BlockSpec((rows, tt), lambda i, j: (0, i))
    vec = pl.BlockSpec((1, D_MODEL), lambda i, j: (0, 0))
    return pl.pallas_call(
        _peer_expert_kernel,
        grid=(t // tt, PEER_N_EXPERTS // eb),
        in_specs=[pl.BlockSpec((tt, D_MODEL), lambda i, j: (i, 0)),
                  route, route, route, route,
                  pl.BlockSpec((eb, D_MODEL), lambda i, j: (j, 0)),
                  pl.BlockSpec((D_MODEL, eb), lambda i, j: (0, j)),
                  vec, vec],
        out_specs=pl.BlockSpec((tt, D_MODEL), lambda i, j: (i, 0)),
        out_shape=jax.ShapeDtypeStruct((t, D_MODEL), F32),
        scratch_shapes=[pltpu.VMEM((D_MODEL, tt), F32), pltpu.VMEM((eb, tt), BF)],
        compiler_params=_cparams(("parallel", "arbitrary")),
        name="peer_expert_ln",
    )(x2d, e1, cnt, e2, rk2, u, vt, g, bb)


def _split_w_in(w):
    sizes = (256, 256, 256, MLA_Q_RANK, MLA_KV_RANK, MLA_ROPE_DIM, 384, 384, 384)
    out, off = [], 0
    for n in sizes:
        out.append(w[:, off:off + n])
        off += n
    return out


def _pad_rows(w, n_heads, d):
    n = w.shape[1]
    w3 = w.reshape(n_heads, d, n)
    return jnp.zeros((n_heads, LANES, n), w.dtype).at[:, :d, :].set(w3).reshape(n_heads * LANES, n)


def _layer(x2d, batch, seq, l, w_in, diff_lambda, diff_subln_g, mla_q_norm_g, mla_kv_norm_g, mla_w_uq,
           mla_w_ukv, w_o, ln1_g, ln1_b, peer_w_q, peer_sub_keys, peer_u, peer_v, ln2_g, ln2_b):
    lambda_init = 0.8 - 0.6 * math.exp(-0.3 * (l + 1))
    a_q, a_k, a_v, b_cq, b_ckv, b_kr, c_q, c_k, c_v = _split_w_in(w_in)
    bf = lambda a: a.astype(BF)

    cq, sq = _rope_tables(seq, DIFF_QK_DIM, 0, DIFF_QK_DIM ** -0.5, 0.0)
    ck, sk = _rope_tables(seq, DIFF_QK_DIM, 0, 1.0, 0.0)
    nh = 2 * DIFF_HEADS
    qa, ka, va = _proj_qkv(
        x2d, seq,
        bf(_pad_heads(a_q, nh, DIFF_QK_DIM)), bf(_pad_heads(_rot_cols(a_q, nh, DIFF_QK_DIM), nh, DIFF_QK_DIM)),
        bf(_pad_heads(a_k, nh, DIFF_QK_DIM)), bf(_pad_heads(_rot_cols(a_k, nh, DIFF_QK_DIM), nh, DIFF_QK_DIM)),
        bf(_pad_heads(a_v, DIFF_HEADS, DIFF_V_DIM)), cq, sq, ck, sk)
    g_pad = jnp.zeros((1, LANES), F32).at[0, :DIFF_V_DIM].set(diff_subln_g.astype(F32))
    sh = lambda a: a.reshape(batch, seq, a.shape[-1])
    out_a = _diff_attn(sh(qa), sh(ka), sh(va), diff_lambda.astype(F32), g_pad, lambda_init)

    qk_dim = MLA_NOPE_DIM + MLA_ROPE_DIM
    cqm, sqm = _rope_tables(seq, MLA_ROPE_DIM, MLA_NOPE_DIM, qk_dim ** -0.5, 1.0)
    cqm = cqm.at[:, qk_dim:].set(0.0)
    ckm, skm = _rope_tables(seq, MLA_ROPE_DIM, MLA_NOPE_DIM, 1.0, 0.0)
    uq3 = mla_w_uq.reshape(MLA_Q_RANK, MLA_HEADS, qk_dim)
    uq_rope = uq3[..., MLA_NOPE_DIM:].reshape(MLA_Q_RANK, MLA_HEADS * MLA_ROPE_DIM)
    uq_rot = _pad_heads(_rot_cols(uq_rope, MLA_HEADS, MLA_ROPE_DIM), MLA_HEADS, MLA_ROPE_DIM, MLA_NOPE_DIM)
    ukv3 = mla_w_ukv.reshape(MLA_KV_RANK, MLA_HEADS, MLA_NOPE_DIM + MLA_V_DIM)
    uk = ukv3[..., :MLA_NOPE_DIM].reshape(MLA_KV_RANK, MLA_HEADS * MLA_NOPE_DIM)
    uv = ukv3[..., MLA_NOPE_DIM:].reshape(MLA_KV_RANK, MLA_HEADS * MLA_V_DIM)
    qb, kb, vb = _proj_mla(
        x2d, seq, bf(b_cq), bf(b_ckv),
        bf(_pad_heads(b_kr, 1, MLA_ROPE_DIM, MLA_NOPE_DIM)),
        bf(_pad_heads(_rot_cols(b_kr, 1, MLA_ROPE_DIM), 1, MLA_ROPE_DIM, MLA_NOPE_DIM)),
        mla_q_norm_g.astype(F32)[None, :], mla_kv_norm_g.astype(F32)[None, :],
        bf(_pad_heads(mla_w_uq, MLA_HEADS, qk_dim)), bf(uq_rot),
        bf(_pad_heads(uk, MLA_HEADS, MLA_NOPE_DIM)), bf(_pad_heads(uv, MLA_HEADS, MLA_V_DIM)),
        cqm, sqm, ckm, skm)
    out_b = _mla_attn(sh(qb), sh(kb), sh(vb))

    cqd, sqd = _rope_tables(seq, DIL_HEAD_DIM, 0, DIL_HEAD_DIM ** -0.5, 0.0)
    ckd, skd = _rope_tables(seq, DIL_HEAD_DIM, 0, 1.0, 0.0)
    nh = DIL_HEADS
    qc, kc, vc = _proj_qkv(
        x2d, seq,
        bf(_pad_heads(c_q, nh, DIL_HEAD_DIM)), bf(_pad_heads(_rot_cols(c_q, nh, DIL_HEAD_DIM), nh, DIL_HEAD_DIM)),
        bf(_pad_heads(c_k, nh, DIL_HEAD_DIM)), bf(_pad_heads(_rot_cols(c_k, nh, DIL_HEAD_DIM), nh, DIL_HEAD_DIM)),
        bf(_pad_heads(c_v, nh, DIL_HEAD_DIM)), cqd, sqd, ckd, skd)
    band = [_band_attn(sh(qc), sh(kc), sh(vc), dil) for window, dil in DIL_PAIRS]
    fl = lambda a: a.reshape(batch * seq, a.shape[-1])

    wo_a = _pad_rows(w_o[:DIFF_HEADS * DIFF_V_DIM], DIFF_HEADS, DIFF_V_DIM)
    wo_b = _pad_rows(w_o[256:256 + MLA_HEADS * MLA_V_DIM], MLA_HEADS, MLA_V_DIM)
    wo_c = _pad_rows(w_o[640:], DIL_HEADS, DIL_HEAD_DIM)
    x1 = _out_proj(x2d, fl(out_a), fl(out_b),
                   fl(band[0][0]), fl(band[1][0]), fl(band[2][0]),
                   fl(band[0][1]), fl(band[1][1]), fl(band[2][1]),
                   bf(wo_a), bf(wo_b), bf(wo_c), ln1_g.astype(F32)[None, :], ln1_b.astype(F32)[None, :])

    keys = bf(peer_sub_keys.reshape(PEER_HEADS * 2, PEER_N_KEYS, PEER_HALF_DIM))
    e1, cnt, e2, rk2 = _peer_route(x1, bf(peer_w_q), keys)
    return _peer_expert(x1, e1, cnt, e2, rk2, bf(peer_u), bf(peer_v).T,
                        ln2_g.astype(F32)[None, :], ln2_b.astype(F32)[None, :])


def kernel(x, w_in, diff_lambda, diff_subln_g, mla_q_norm_g, mla_kv_norm_g, mla_w_uq, mla_w_ukv, w_o,
           ln1_g, ln1_b, peer_w_q, peer_sub_keys, peer_u, peer_v, ln2_g, ln2_b):
    batch, seq, d = x.shape
    params = (w_in, diff_lambda, diff_subln_g, mla_q_norm_g, mla_kv_norm_g, mla_w_uq, mla_w_ukv, w_o,
              ln1_g, ln1_b, peer_w_q, peer_sub_keys, peer_u, peer_v, ln2_g, ln2_b)
    x2d = x.reshape(batch * seq, d)
    for l in range(DEPTH):
        x2d = _layer(x2d, batch, seq, l, *[p[l] for p in params])
    return x2d.reshape(batch, seq, d)
```

```python
import functools
import math

import jax
import jax.numpy as jnp
from jax import lax
from jax.experimental import pallas as pl
from jax.experimental.pallas import tpu as pltpu

BF = jnp.bfloat16
F32 = jnp.float32

LANES = 128
BF_ROWS = 16
D_MODEL = 1024
DEPTH = 2
DIFF_HEADS = 4
DIFF_QK_DIM = 32
DIFF_V_DIM = 64
MLA_HEADS = 6
MLA_Q_RANK = 256
MLA_KV_RANK = 128
MLA_NOPE_DIM = 64
MLA_ROPE_DIM = 32
MLA_V_DIM = 64
DIL_HEADS = 6
DIL_HEAD_DIM = 64
DIL_PAIRS = ((128, 1), (512, 4), (2048, 16))
ROPE_THETA = 10000.0
PEER_HEADS = 8
PEER_N_KEYS = 128
PEER_N_EXPERTS = PEER_N_KEYS * PEER_N_KEYS
PEER_HALF_DIM = 128
PEER_TOPK = 16
DEEPNORM_ALPHA = (2 * DEPTH) ** 0.25
LN_EPS = 1e-5
NEG_INF = -1e30

VMEM_LIMIT = 56 * 1024 * 1024

TM_PROJ = 512
TQ_DENSE = 256
TQ_BAND = 256
BAND_HALF = 64
TT_PEER = 512
EB_PEER = 1024
ROW_GROUP = 2


def _cparams(sem):
    return pltpu.CompilerParams(dimension_semantics=sem, vmem_limit_bytes=VMEM_LIMIT)


def _dot(a, b):
    return jnp.dot(a, b, preferred_element_type=F32)


def _dot_nt(a, b):
    return lax.dot_general(a, b, (((1,), (1,)), ((), ())), preferred_element_type=F32)


def _pad_heads(w, n_heads, d, offset=0):
    k = w.shape[0]
    w3 = w.reshape(k, n_heads, d)
    out = jnp.zeros((k, n_heads, LANES), w.dtype).at[:, :, offset:offset + d].set(w3)
    return out.reshape(k, n_heads * LANES)


def _rot_cols(w, n_heads, d):
    k = w.shape[0]
    w3 = w.reshape(k, n_heads, d)
    half = d // 2
    return jnp.concatenate([-w3[..., half:], w3[..., :half]], axis=-1).reshape(k, n_heads * d)


def _rope_tables(seq, d, offset, scale, fill):
    half = d // 2
    inv_freq = jnp.exp(-math.log(ROPE_THETA) * jnp.arange(half, dtype=F32) / half)
    ang = jnp.arange(seq, dtype=jnp.int32).astype(F32)[:, None] * inv_freq[None, :]
    cos, sin = jnp.cos(ang), jnp.sin(ang)
    cos_t = jnp.full((seq, LANES), fill, F32).at[:, offset:offset + d].set(jnp.concatenate([cos, cos], -1))
    sin_t = jnp.zeros((seq, LANES), F32).at[:, offset:offset + d].set(jnp.concatenate([sin, sin], -1))
    return cos_t * scale, sin_t * scale


def _rope_store(o_ref, h, hr, cos, sin):
    for s in range(o_ref.shape[1] // LANES):
        sl = slice(s * LANES, (s + 1) * LANES)
        o_ref[:, sl] = (h[:, sl] * cos + hr[:, sl] * sin).astype(o_ref.dtype)


def _proj_qkv_kernel(x_ref, wq_ref, wqr_ref, wk_ref, wkr_ref, wv_ref, cq_ref, sq_ref, ck_ref, sk_ref,
                     q_ref, k_ref, v_ref):
    xb = x_ref[...].astype(BF)
    _rope_store(q_ref, _dot(xb, wq_ref[...]), _dot(xb, wqr_ref[...]), cq_ref[...], sq_ref[...])
    _rope_store(k_ref, _dot(xb, wk_ref[...]), _dot(xb, wkr_ref[...]), ck_ref[...], sk_ref[...])
    v_ref[...] = _dot(xb, wv_ref[...]).astype(v_ref.dtype)


def _proj_qkv(x2d, seq, wq, wqr, wk, wkr, wv, cq, sq, ck, sk):
    t = x2d.shape[0]
    tm = TM_PROJ
    nq, nv = wq.shape[1], wv.shape[1]
    pos_blocks = seq // tm
    full = lambda a: pl.BlockSpec(a.shape, lambda i: (0, 0))
    tab = pl.BlockSpec((tm, LANES), lambda i: (i % pos_blocks, 0))
    return pl.pallas_call(
        _proj_qkv_kernel,
        grid=(t // tm,),
        in_specs=[pl.BlockSpec((tm, D_MODEL), lambda i: (i, 0)),
                  full(wq), full(wqr), full(wk), full(wkr), full(wv), tab, tab, tab, tab],
        out_specs=[pl.BlockSpec((tm, nq), lambda i: (i, 0)),
                   pl.BlockSpec((tm, nq), lambda i: (i, 0)),
                   pl.BlockSpec((tm, nv), lambda i: (i, 0))],
        out_shape=[jax.ShapeDtypeStruct((t, nq), BF), jax.ShapeDtypeStruct((t, nq), BF),
                   jax.ShapeDtypeStruct((t, nv), BF)],
        compiler_params=_cparams(("parallel",)),
        name="proj_qkv",
    )(x2d, wq, wqr, wk, wkr, wv, cq, sq, ck, sk)


def _rms(x, g, n):
    return x * lax.rsqrt(jnp.sum(x * x, axis=-1, keepdims=True) / n + LN_EPS) * g


def _proj_mla_kernel(x_ref, wcq_ref, wckv_ref, wkr_ref, wkrr_ref, gq_ref, gkv_ref,
                     wuq_ref, wuqr_ref, wuk_ref, wuv_ref, cq_ref, sq_ref, ck_ref, sk_ref,
                     q_ref, k_ref, v_ref):
    xb = x_ref[...].astype(BF)
    nq = _rms(_dot(xb, wcq_ref[...]), gq_ref[...], MLA_Q_RANK).astype(BF)
    nkv = _rms(_dot(xb, wckv_ref[...]), gkv_ref[...], MLA_KV_RANK).astype(BF)
    _rope_store(q_ref, _dot(nq, wuq_ref[...]), _dot(nq, wuqr_ref[...]), cq_ref[...], sq_ref[...])
    kpe = _dot(xb, wkr_ref[...]) * ck_ref[...] + _dot(xb, wkrr_ref[...]) * sk_ref[...]
    kn = _dot(nkv, wuk_ref[...])
    for s in range(MLA_HEADS):
        sl = slice(s * LANES, (s + 1) * LANES)
        k_ref[:, sl] = (kn[:, sl] + kpe).astype(k_ref.dtype)
    v_ref[...] = _dot(nkv, wuv_ref[...]).astype(v_ref.dtype)


def _proj_mla(x2d, seq, wcq, wckv, wkr, wkrr, gq, gkv, wuq, wuqr, wuk, wuv, cq, sq, ck, sk):
    t = x2d.shape[0]
    tm = TM_PROJ
    n = MLA_HEADS * LANES
    pos_blocks = seq // tm
    full = lambda a: pl.BlockSpec(a.shape, lambda i: (0, 0))
    tab = pl.BlockSpec((tm, LANES), lambda i: (i % pos_blocks, 0))
    out = pl.BlockSpec((tm, n), lambda i: (i, 0))
    return pl.pallas_call(
        _proj_mla_kernel,
        grid=(t // tm,),
        in_specs=[pl.BlockSpec((tm, D_MODEL), lambda i: (i, 0)),
                  full(wcq), full(wckv), full(wkr), full(wkrr), full(gq), full(gkv),
                  full(wuq), full(wuqr), full(wuk), full(wuv), tab, tab, tab, tab],
        out_specs=[out, out, out],
        out_shape=[jax.ShapeDtypeStruct((t, n), BF)] * 3,
        compiler_params=_cparams(("parallel",)),
        name="proj_mla",
    )(x2d, wcq, wckv, wkr, wkrr, gq, gkv, wuq, wuqr, wuk, wuv, cq, sq, ck, sk)


def _softmax_pv(q, k, v):
    s = _dot_nt(q, k)
    m = jnp.max(s, axis=-1, keepdims=True)
    p = jnp.exp(s - m)
    l = jnp.sum(p, axis=-1, keepdims=True)
    return _dot(p.astype(BF), v) / l


def _diff_attn_kernel(lam_ref, g_ref, q1_ref, q2_ref, k1_ref, k2_ref, v_ref, o_ref, *, lambda_init):
    lam = lam_ref[...]
    lam_full = (jnp.exp(jnp.sum(lam[0:1] * lam[1:2], axis=-1, keepdims=True))
                - jnp.exp(jnp.sum(lam[2:3] * lam[3:4], axis=-1, keepdims=True)) + lambda_init)
    v = v_ref[0]
    o1 = _softmax_pv(q1_ref[0], k1_ref[0], v)
    o2 = _softmax_pv(q2_ref[0], k2_ref[0], v)
    o = o1 - lam_full * o2
    o_ref[0] = (_rms(o, g_ref[...], DIFF_V_DIM) * (1.0 - lambda_init)).astype(o_ref.dtype)


def _diff_attn(q, k, v, lam, g_pad, lambda_init):
    b, s, _ = q.shape
    tq = TQ_DENSE
    qspec = lambda off: pl.BlockSpec((1, tq, LANES), lambda bi, h, i: (bi, i, 2 * h + off))
    kspec = lambda off: pl.BlockSpec((1, s, LANES), lambda bi, h, i: (bi, 0, 2 * h + off))
    return pl.pallas_call(
        functools.partial(_diff_attn_kernel, lambda_init=lambda_init),
        grid=(b, DIFF_HEADS, s // tq),
        in_specs=[pl.BlockSpec(lam.shape, lambda bi, h, i: (0, 0)),
                  pl.BlockSpec(g_pad.shape, lambda bi, h, i: (0, 0)),
                  qspec(0), qspec(1), kspec(0), kspec(1),
                  pl.BlockSpec((1, s, LANES), lambda bi, h, i: (bi, 0, h))],
        out_specs=pl.BlockSpec((1, tq, LANES), lambda bi, h, i: (bi, i, h)),
        out_shape=jax.ShapeDtypeStruct((b, s, DIFF_HEADS * LANES), BF),
        compiler_params=_cparams(("parallel", "parallel", "arbitrary")),
        name="diff_attn",
    )(lam, g_pad, q, q, k, k, v)


def _mla_attn_kernel(q_ref, k_ref, v_ref, o_ref):
    o_ref[0] = _softmax_pv(q_ref[0], k_ref[0], v_ref[0]).astype(o_ref.dtype)


def _mla_attn(q, k, v):
    b, s, _ = q.shape
    tq = TQ_DENSE
    kv = pl.BlockSpec((1, s, LANES), lambda bi, h, i: (bi, 0, h))
    qo = pl.BlockSpec((1, tq, LANES), lambda bi, h, i: (bi, i, h))
    return pl.pallas_call(
        _mla_attn_kernel,
        grid=(b, MLA_HEADS, s // tq),
        in_specs=[qo, kv, kv],
        out_specs=qo,
        out_shape=jax.ShapeDtypeStruct((b, s, MLA_HEADS * LANES), BF),
        compiler_params=_cparams(("parallel", "parallel", "arbitrary")),
        name="mla_attn",
    )(q, k, v)


def _band_attn_kernel(q_ref, kp_ref, kc_ref, kn_ref, vp_ref, vc_ref, vn_ref, o_ref, lse_ref, *, seg_len):
    i = pl.program_id(2)
    tq = q_ref.shape[1]
    blk = LANES
    span = 3 * blk
    kk = jnp.concatenate([kp_ref[0], kc_ref[0], kn_ref[0]], axis=0)
    vv = jnp.concatenate([vp_ref[0], vc_ref[0], vn_ref[0]], axis=0)
    qi = lax.broadcasted_iota(jnp.int32, (blk, span), 0)
    kj = lax.broadcasted_iota(jnp.int32, (blk, span), 1)
    for j in range(tq // blk):
        qpos = i * tq + j * blk + qi
        kpos = i * tq + (j - 1) * blk + kj
        mask = (jnp.abs(kpos - qpos) <= BAND_HALF) & (kpos >= 0) & (kpos < seg_len)
        for h in range(DIL_HEADS):
            cs = slice(h * LANES, (h + 1) * LANES)
            q = q_ref[0, j * blk:(j + 1) * blk, cs]
            k = kk[j * blk:j * blk + span, cs]
            v = vv[j * blk:j * blk + span, cs]
            s = jnp.where(mask, _dot_nt(q, k), NEG_INF)
            m = jnp.max(s, axis=-1, keepdims=True)
            p = jnp.exp(s - m)
            l = jnp.sum(p, axis=-1, keepdims=True)
            o_ref[0, j * blk:(j + 1) * blk, cs] = _dot(p.astype(BF), v) / l
            lse_ref[0, j * blk:(j + 1) * blk, cs] = jnp.broadcast_to(m + jnp.log(l), (blk, LANES))


def _band_attn(q, k, v, dil):
    b, s, w = q.shape
    seg = s // dil
    tq = TQ_BAND
    r = tq // LANES
    nb = seg // LANES
    view = lambda a: a.reshape(b, seg, dil * w)
    cur = pl.BlockSpec((1, tq, w), lambda bi, c, i: (bi, i, c))
    prev = pl.BlockSpec((1, LANES, w), lambda bi, c, i: (bi, jnp.maximum(i * r - 1, 0), c))
    nxt = pl.BlockSpec((1, LANES, w), lambda bi, c, i: (bi, jnp.minimum((i + 1) * r, nb - 1), c))
    o, lse = pl.pallas_call(
        functools.partial(_band_attn_kernel, seg_len=seg),
        grid=(b, dil, seg // tq),
        in_specs=[cur, prev, cur, nxt, prev, cur, nxt],
        out_specs=[cur, cur],
        out_shape=[jax.ShapeDtypeStruct((b, seg, dil * w), F32)] * 2,
        compiler_params=_cparams(("parallel", "parallel", "arbitrary")),
        name="band_attn",
    )(view(q), view(k), view(k), view(k), view(v), view(v), view(v))
    return o.reshape(b, s, w), lse.reshape(b, s, w)


def _layer_norm(y, g, b):
    mu = jnp.mean(y, axis=-1, keepdims=True)
    yc = y - mu
    var = jnp.mean(yc * yc, axis=-1, keepdims=True)
    return yc * lax.rsqrt(var + LN_EPS) * g + b


def _out_proj_kernel(x_ref, oa_ref, ob_ref, o1_ref, o2_ref, o3_ref, l1_ref, l2_ref, l3_ref,
                     wa_ref, wb_ref, wc_ref, g_ref, b_ref, y_ref):
    l1, l2, l3 = l1_ref[...], l2_ref[...], l3_ref[...]
    m = jnp.maximum(jnp.maximum(l1, l2), l3)
    e1, e2, e3 = jnp.exp(l1 - m), jnp.exp(l2 - m), jnp.exp(l3 - m)
    oc = (e1 * o1_ref[...] + e2 * o2_ref[...] + e3 * o3_ref[...]) / (e1 + e2 + e3)
    mix = (_dot(oa_ref[...], wa_ref[...]) + _dot(ob_ref[...], wb_ref[...])
           + _dot(oc.astype(BF), wc_ref[...]))
    y_ref[...] = _layer_norm(DEEPNORM_ALPHA * x_ref[...] + mix, g_ref[...], b_ref[...])


def _out_proj(x2d, oa, ob, o1, o2, o3, l1, l2, l3, wa, wb, wc, g, bb):
    t = x2d.shape[0]
    tm = TM_PROJ
    row = lambda a: pl.BlockSpec((tm, a.shape[1]), lambda i: (i, 0))
    full = lambda a: pl.BlockSpec(a.shape, lambda i: (0, 0))
    return pl.pallas_call(
        _out_proj_kernel,
        grid=(t // tm,),
        in_specs=[row(x2d), row(oa), row(ob), row(o1), row(o2), row(o3), row(l1), row(l2), row(l3),
                  full(wa), full(wb), full(wc), full(g), full(bb)],
        out_specs=row(x2d),
        out_shape=jax.ShapeDtypeStruct((t, D_MODEL), F32),
        compiler_params=_cparams(("parallel",)),
        name="out_proj_ln",
    )(x2d, oa, ob, o1, o2, o3, l1, l2, l3, wa, wb, wc, g, bb)


def _extract16(s, want_rank):
    row = lax.broadcasted_iota(jnp.int32, (PEER_TOPK, s.shape[1]), 0)
    work, top = s, jnp.zeros((PEER_TOPK, s.shape[1]), F32)
    rank = jnp.full(s.shape, float(PEER_TOPK), F32)
    for kk in range(PEER_TOPK):
        m = jnp.max(work, axis=0, keepdims=True)
        sel = work == m
        work = jnp.where(sel, -jnp.inf, work)
        top = jnp.where(row == kk, m, top)
        if want_rank:
            rank = jnp.where(sel, float(kk), rank)
    return top, rank


def _route_chunk(s1, s2):
    t1, _ = _extract16(s1, False)
    t2, rank2 = _extract16(s2, True)
    cand = jnp.concatenate(
        [t1[0:1] + t2]
        + [t1[k1:k1 + 1] + t2[0:8] for k1 in range(1, 8)]
        + [t1[8:16] + t2[0:1]], axis=0)
    c0 = t1[0:1] + t2[0:1]
    work, z, tau = cand, jnp.zeros_like(c0), c0
    for _ in range(PEER_TOPK):
        tau = jnp.max(work, axis=0, keepdims=True)
        work = jnp.where(work == tau, -jnp.inf, work)
        z = z + jnp.exp(tau - c0)
    cnt_of_rank = jnp.zeros_like(t1)
    for k2 in range(PEER_TOPK):
        cnt_of_rank = cnt_of_rank + jnp.where(t1 + t2[k2:k2 + 1] >= tau, 1.0, 0.0)
    cnt = jnp.zeros_like(s1)
    for k1 in range(PEER_TOPK):
        cnt = jnp.where(s1 == t1[k1:k1 + 1], cnt_of_rank[k1:k1 + 1], cnt)
    e1 = jnp.exp(s1 - t1[0:1]) / z
    e2 = jnp.exp(s2 - t2[0:1])
    return e1, cnt, e2, rank2


def _peer_route_kernel(x_ref, wq_ref, keys_ref, e1_ref, cnt_ref, e2_ref, rk2_ref, q_scr, s_scr):
    nk = PEER_N_KEYS
    n_chunks = e1_ref.shape[0]
    q = _dot(x_ref[...].astype(BF), wq_ref[...]).astype(BF)
    for s in range(2 * PEER_HEADS):
        q_scr[s] = q[:, s * nk:(s + 1) * nk]

    def head_body(h, carry):
        s1 = _dot_nt(keys_ref[2 * h], q_scr[2 * h])
        s2 = _dot_nt(keys_ref[2 * h + 1], q_scr[2 * h + 1])
        for c in range(n_chunks):
            s_scr[0, c] = s1[:, c * LANES:(c + 1) * LANES]
            s_scr[1, c] = s2[:, c * LANES:(c + 1) * LANES]
        rows = pl.ds(pl.multiple_of(h * nk, nk), nk)

        def chunk_body(c, carry2):
            e1, cnt, e2, rank2 = _route_chunk(s_scr[0, c], s_scr[1, c])
            e1_ref[c, rows, :] = e1
            cnt_ref[c, rows, :] = cnt
            e2_ref[c, rows, :] = e2.astype(e2_ref.dtype)
            rk2_ref[c, rows, :] = rank2.astype(rk2_ref.dtype)
            return carry2

        lax.fori_loop(0, n_chunks, chunk_body, 0)
        return carry

    lax.fori_loop(0, PEER_HEADS, head_body, 0)


def _peer_route(x2d, wq, keys):
    t = x2d.shape[0]
    tt = TT_PEER
    nc = tt // LANES
    rows = PEER_HEADS * PEER_N_KEYS
    out = pl.BlockSpec((nc, rows, LANES), lambda i: (i, 0, 0))
    shape = lambda dt: jax.ShapeDtypeStruct((t // LANES, rows, LANES), dt)
    return pl.pallas_call(
        _peer_route_kernel,
        grid=(t // tt,),
        in_specs=[pl.BlockSpec((tt, D_MODEL), lambda i: (i, 0)),
                  pl.BlockSpec(wq.shape, lambda i: (0, 0)),
                  pl.BlockSpec(keys.shape, lambda i: (0, 0, 0))],
        out_specs=[out] * 4,
        out_shape=[shape(F32)] * 4,
        scratch_shapes=[pltpu.VMEM((2 * PEER_HEADS, tt, PEER_N_KEYS), BF),
                        pltpu.VMEM((2, nc, PEER_N_KEYS, LANES), F32)],
        compiler_params=_cparams(("parallel",)),
        name="peer_route",
    )(x2d, wq, keys)


def _gelu(x):
    return 0.5 * x * (1.0 + lax.erf(x * (2.0 ** -0.5)))


def _peer_expert_kernel(x_ref, e1_ref, cnt_ref, e2_ref, rk2_ref, u_ref, vt_ref, g_ref, b_ref,
                        y_ref, acc_ref, a_ref, xb_ref, ht_ref):
    j = pl.program_id(1)
    nk = PEER_N_KEYS
    rows_per_step = u_ref.shape[0] // nk
    n_chunks = e1_ref.shape[0]

    @pl.when(j == 0)
    def _():
        acc_ref[...] = jnp.zeros_like(acc_ref)
        xb_ref[...] = x_ref[...].astype(BF)

    ht_ref[...] = _dot_nt(u_ref[...], xb_ref[...])
    n_b = nk // BF_ROWS
    for c in range(n_chunks):
        cs = slice(c * LANES, (c + 1) * LANES)
        for r0 in range(0, rows_per_step, ROW_GROUP):
            w = [[jnp.zeros((BF_ROWS, LANES), F32)] * n_b for _ in range(ROW_GROUP)]
            for h in range(PEER_HEADS):
                a = h * nk + j * rows_per_step + r0
                e1 = [jnp.broadcast_to(e1_ref[c, pl.ds(a + r, 1), :], (BF_ROWS, LANES))
                      for r in range(ROW_GROUP)]
                cnt = [jnp.broadcast_to(cnt_ref[c, pl.ds(a + r, 1), :], (BF_ROWS, LANES))
                       for r in range(ROW_GROUP)]
                for b in range(n_b):
                    hs = slice(h * nk + b * BF_ROWS, h * nk + (b + 1) * BF_ROWS)
                    e2 = e2_ref[c, hs, :]
                    rk2 = rk2_ref[c, hs, :]
                    for r in range(ROW_GROUP):
                        w[r][b] = w[r][b] + jnp.where(rk2 < cnt[r], e1[r] * e2, 0.0)
            for r in range(ROW_GROUP):
                for b in range(n_b):
                    bs = slice((r0 + r) * nk + b * BF_ROWS, (r0 + r) * nk + (b + 1) * BF_ROWS)
                    a_ref[bs, cs] = (w[r][b] * _gelu(ht_ref[bs, cs])).astype(BF)
    acc_ref[...] += _dot(vt_ref[...], a_ref[...])

    @pl.when(j == pl.num_programs(1) - 1)
    def _():
        ffn = acc_ref[...].T
        y_ref[...] = _layer_norm(DEEPNORM_ALPHA * x_ref[...] + ffn, g_ref[...], b_ref[...])


def _peer_expert(x2d, e1, cnt, e2, rk2, u, vt, g, bb):
    t = x2d.shape[0]
    tt, eb = TT_PEER, EB_PEER
    nc = tt // LANES
    rows = PEER_HEADS * PEER_N_KEYS
    route = pl.BlockSpec((nc, rows, LANES), lambda i, j: (i, 0, 0))
    vec = pl.BlockSpec((1, D_MODEL), lambda i, j: (0, 0))
    return pl.pallas_call(
        _peer_expert_kernel,
        grid=(t // tt, PEER_N_EXPERTS // eb),
        in_specs=[pl.BlockSpec((tt, D_MODEL), lambda i, j: (i, 0)),
                  route, route, route, route,
                  pl.BlockSpec((eb, D_MODEL), lambda i, j: (j, 0)),
                  pl.BlockSpec((D_MODEL, eb), lambda i, j: (0, j)),
                  vec, vec],
        out_specs=pl.BlockSpec((tt, D_MODEL), lambda i, j: (i, 0)),
        out_shape=jax.ShapeDtypeStruct((t, D_MODEL), F32),
        scratch_shapes=[pltpu.VMEM((D_MODEL, tt), F32), pltpu.VMEM((eb, tt), BF),
                        pltpu.VMEM((tt, D_MODEL), BF),
                        pltpu.VMEM((eb, tt), F32)],
        compiler_params=_cparams(("parallel", "arbitrary")),
        name="peer_expert_ln",
    )(x2d, e1, cnt, e2, rk2, u, vt, g, bb)


def _split_w_in(w):
    sizes = (256, 256, 256, MLA_Q_RANK, MLA_KV_RANK, MLA_ROPE_DIM, 384, 384, 384)
    out, off = [], 0
    for n in sizes:
        out.append(w[:, off:off + n])
        off += n
    return out


def _pad_rows(w, n_heads, d):
    n = w.shape[1]
    w3 = w.reshape(n_heads, d, n)
    return jnp.zeros((n_heads, LANES, n), w.dtype).at[:, :d, :].set(w3).reshape(n_heads * LANES, n)


def _layer(x2d, batch, seq, l, w_in, diff_lambda, diff_subln_g, mla_q_norm_g, mla_kv_norm_g, mla_w_uq,
           mla_w_ukv, w_o, ln1_g, ln1_b, peer_w_q, peer_sub_keys, peer_u, peer_v, ln2_g, ln2_b):
    lambda_init = 0.8 - 0.6 * math.exp(-0.3 * (l + 1))
    a_q, a_k, a_v, b_cq, b_ckv, b_kr, c_q, c_k, c_v = _split_w_in(w_in)
    bf = lambda a: a.astype(BF)

    cq, sq = _rope_tables(seq, DIFF_QK_DIM, 0, DIFF_QK_DIM ** -0.5, 0.0)
    ck, sk = _rope_tables(seq, DIFF_QK_DIM, 0, 1.0, 0.0)
    nh = 2 * DIFF_HEADS
    qa, ka, va = _proj_qkv(
        x2d, seq,
        bf(_pad_heads(a_q, nh, DIFF_QK_DIM)), bf(_pad_heads(_rot_cols(a_q, nh, DIFF_QK_DIM), nh, DIFF_QK_DIM)),
        bf(_pad_heads(a_k, nh, DIFF_QK_DIM)), bf(_pad_heads(_rot_cols(a_k, nh, DIFF_QK_DIM), nh, DIFF_QK_DIM)),
        bf(_pad_heads(a_v, DIFF_HEADS, DIFF_V_DIM)), cq, sq, ck, sk)
    g_pad = jnp.zeros((1, LANES), F32).at[0, :DIFF_V_DIM].set(diff_subln_g.astype(F32))
    sh = lambda a: a.reshape(batch, seq, a.shape[-1])
    out_a = _diff_attn(sh(qa), sh(ka), sh(va), diff_lambda.astype(F32), g_pad, lambda_init)

    qk_dim = MLA_NOPE_DIM + MLA_ROPE_DIM
    cqm, sqm = _rope_tables(seq, MLA_ROPE_DIM, MLA_NOPE_DIM, qk_dim ** -0.5, 1.0)
    cqm = cqm.at[:, qk_dim:].set(0.0)
    ckm, skm = _rope_tables(seq, MLA_ROPE_DIM, MLA_NOPE_DIM, 1.0, 0.0)
    uq3 = mla_w_uq.reshape(MLA_Q_RANK, MLA_HEADS, qk_dim)
    uq_rope = uq3[..., MLA_NOPE_DIM:].reshape(MLA_Q_RANK, MLA_HEADS * MLA_ROPE_DIM)
    uq_rot = _pad_heads(_rot_cols(uq_rope, MLA_HEADS, MLA_ROPE_DIM), MLA_HEADS, MLA_ROPE_DIM, MLA_NOPE_DIM)
    ukv3 = mla_w_ukv.reshape(MLA_KV_RANK, MLA_HEADS, MLA_NOPE_DIM + MLA_V_DIM)
    uk = ukv3[..., :MLA_NOPE_DIM].reshape(MLA_KV_RANK, MLA_HEADS * MLA_NOPE_DIM)
    uv = ukv3[..., MLA_NOPE_DIM:].reshape(MLA_KV_RANK, MLA_HEADS * MLA_V_DIM)
    qb, kb, vb = _proj_mla(
        x2d, seq, bf(b_cq), bf(b_ckv),
        bf(_pad_heads(b_kr, 1, MLA_ROPE_DIM, MLA_NOPE_DIM)),
        bf(_pad_heads(_rot_cols(b_kr, 1, MLA_ROPE_DIM), 1, MLA_ROPE_DIM, MLA_NOPE_DIM)),
        mla_q_norm_g.astype(F32)[None, :], mla_kv_norm_g.astype(F32)[None, :],
        bf(_pad_heads(mla_w_uq, MLA_HEADS, qk_dim)), bf(uq_rot),
        bf(_pad_heads(uk, MLA_HEADS, MLA_NOPE_DIM)), bf(_pad_heads(uv, MLA_HEADS, MLA_V_DIM)),
        cqm, sqm, ckm, skm)
    out_b = _mla_attn(sh(qb), sh(kb), sh(vb))

    cqd, sqd = _rope_tables(seq, DIL_HEAD_DIM, 0, DIL_HEAD_DIM ** -0.5, 0.0)
    ckd, skd = _rope_tables(seq, DIL_HEAD_DIM, 0, 1.0, 0.0)
    nh = DIL_HEADS
    qc, kc, vc = _proj_qkv(
        x2d, seq,
        bf(_pad_heads(c_q, nh, DIL_HEAD_DIM)), bf(_pad_heads(_rot_cols(c_q, nh, DIL_HEAD_DIM), nh, DIL_HEAD_DIM)),
        bf(_pad_heads(c_k, nh, DIL_HEAD_DIM)), bf(_pad_heads(_rot_cols(c_k, nh, DIL_HEAD_DIM), nh, DIL_HEAD_DIM)),
        bf(_pad_heads(c_v, nh, DIL_HEAD_DIM)), cqd, sqd, ckd, skd)
    band = [_band_attn(sh(qc), sh(kc), sh(vc), dil) for window, dil in DIL_PAIRS]
    fl = lambda a: a.reshape(batch * seq, a.shape[-1])

    wo_a = _pad_rows(w_o[:DIFF_HEADS * DIFF_V_DIM], DIFF_HEADS, DIFF_V_DIM)
    wo_b = _pad_rows(w_o[256:256 + MLA_HEADS * MLA_V_DIM], MLA_HEADS, MLA_V_DIM)
    wo_c = _pad_rows(w_o[640:], DIL_HEADS, DIL_HEAD_DIM)
    x1 = _out_proj(x2d, fl(out_a), fl(out_b),
                   fl(band[0][0]), fl(band[1][0]), fl(band[2][0]),
                   fl(band[0][1]), fl(band[1][1]), fl(band[2][1]),
                   bf(wo_a), bf(wo_b), bf(wo_c), ln1_g.astype(F32)[None, :], ln1_b.astype(F32)[None, :])

    keys = bf(peer_sub_keys.reshape(PEER_HEADS * 2, PEER_N_KEYS, PEER_HALF_DIM))
    e1, cnt, e2, rk2 = _peer_route(x1, bf(peer_w_q), keys)
    return _peer_expert(x1, e1, cnt, e2, rk2, bf(peer_u), bf(peer_v).T,
                        ln2_g.astype(F32)[None, :], ln2_b.astype(F32)[None, :])


def kernel(x, w_in, diff_lambda, diff_subln_g, mla_q_norm_g, mla_kv_norm_g, mla_w_uq, mla_w_ukv, w_o,
           ln1_g, ln1_b, peer_w_q, peer_sub_keys, peer_u, peer_v, ln2_g, ln2_b):
    batch, seq, d = x.shape
    params = (w_in, diff_lambda, diff_subln_g, mla_q_norm_g, mla_kv_norm_g, mla_w_uq, mla_w_ukv, w_o,
              ln1_g, ln1_b, peer_w_q, peer_sub_keys, peer_u, peer_v, ln2_g, ln2_b)
    x2d = x.reshape(batch * seq, d)
    for l in range(DEPTH):
        x2d = _layer(x2d, batch, seq, l, *[p[l] for p in params])
    return x2d.reshape(batch, seq, d)
```

```python
import functools
import math

import jax
import jax.numpy as jnp
from jax import lax
from jax.experimental import pallas as pl
from jax.experimental.pallas import tpu as pltpu

BF = jnp.bfloat16
F32 = jnp.float32

LANES = 128
D_MODEL = 1024
DEPTH = 2
DIFF_HEADS = 4
DIFF_QK_DIM = 32
DIFF_V_DIM = 64
MLA_HEADS = 6
MLA_Q_RANK = 256
MLA_KV_RANK = 128
MLA_NOPE_DIM = 64
MLA_ROPE_DIM = 32
MLA_V_DIM = 64
DIL_HEADS = 6
DIL_HEAD_DIM = 64
DIL_PAIRS = ((128, 1), (512, 4), (2048, 16))
ROPE_THETA = 10000.0
PEER_HEADS = 8
PEER_N_KEYS = 128
PEER_N_EXPERTS = PEER_N_KEYS * PEER_N_KEYS
PEER_HALF_DIM = 128
PEER_TOPK = 16
DEEPNORM_ALPHA = (2 * DEPTH) ** 0.25
LN_EPS = 1e-5
NEG_INF = -1e30
LOG2E = math.log2(math.e)
LN2 = math.log(2.0)

VMEM_LIMIT = 56 * 1024 * 1024

TM_PROJ = 512
TQ_DENSE = 256
TQ_BAND = 256
BAND_HALF = 64
TT_PEER = 512
EB_PEER = 1024


def _cparams(sem):
    return pltpu.CompilerParams(dimension_semantics=sem, vmem_limit_bytes=VMEM_LIMIT)


def _dot(a, b):
    return jnp.dot(a, b, preferred_element_type=F32)


def _dot_nt(a, b):
    return lax.dot_general(a, b, (((1,), (1,)), ((), ())), preferred_element_type=F32)


def _pad_heads(w, n_heads, d, offset=0):
    k = w.shape[0]
    w3 = w.reshape(k, n_heads, d)
    out = jnp.zeros((k, n_heads, LANES), w.dtype).at[:, :, offset:offset + d].set(w3)
    return out.reshape(k, n_heads * LANES)


def _rot_cols(w, n_heads, d):
    k = w.shape[0]
    w3 = w.reshape(k, n_heads, d)
    half = d // 2
    return jnp.concatenate([-w3[..., half:], w3[..., :half]], axis=-1).reshape(k, n_heads * d)


def _rope_tables(seq, d, offset, scale, fill):
    half = d // 2
    inv_freq = jnp.exp(-math.log(ROPE_THETA) * jnp.arange(half, dtype=F32) / half)
    ang = jnp.arange(seq, dtype=jnp.int32).astype(F32)[:, None] * inv_freq[None, :]
    cos, sin = jnp.cos(ang), jnp.sin(ang)
    cos_t = jnp.full((seq, LANES), fill, F32).at[:, offset:offset + d].set(jnp.concatenate([cos, cos], -1))
    sin_t = jnp.zeros((seq, LANES), F32).at[:, offset:offset + d].set(jnp.concatenate([sin, sin], -1))
    return cos_t * scale, sin_t * scale


def _rope_store(o_ref, h, hr, cos, sin):
    for s in range(o_ref.shape[1] // LANES):
        sl = slice(s * LANES, (s + 1) * LANES)
        o_ref[:, sl] = (h[:, sl] * cos + hr[:, sl] * sin).astype(o_ref.dtype)


def _proj_qkv_kernel(x_ref, wq_ref, wqr_ref, wk_ref, wkr_ref, wv_ref, cq_ref, sq_ref, ck_ref, sk_ref,
                     q_ref, k_ref, v_ref):
    xb = x_ref[...].astype(BF)
    _rope_store(q_ref, _dot(xb, wq_ref[...]), _dot(xb, wqr_ref[...]), cq_ref[...], sq_ref[...])
    _rope_store(k_ref, _dot(xb, wk_ref[...]), _dot(xb, wkr_ref[...]), ck_ref[...], sk_ref[...])
    v_ref[...] = _dot(xb, wv_ref[...]).astype(v_ref.dtype)


def _proj_qkv(x2d, seq, wq, wqr, wk, wkr, wv, cq, sq, ck, sk):
    t = x2d.shape[0]
    tm = TM_PROJ
    nq, nv = wq.shape[1], wv.shape[1]
    pos_blocks = seq // tm
    full = lambda a: pl.BlockSpec(a.shape, lambda i: (0, 0))
    tab = pl.BlockSpec((tm, LANES), lambda i: (i % pos_blocks, 0))
    return pl.pallas_call(
        _proj_qkv_kernel,
        grid=(t // tm,),
        in_specs=[pl.BlockSpec((tm, D_MODEL), lambda i: (i, 0)),
                  full(wq), full(wqr), full(wk), full(wkr), full(wv), tab, tab, tab, tab],
        out_specs=[pl.BlockSpec((tm, nq), lambda i: (i, 0)),
                   pl.BlockSpec((tm, nq), lambda i: (i, 0)),
                   pl.BlockSpec((tm, nv), lambda i: (i, 0))],
        out_shape=[jax.ShapeDtypeStruct((t, nq), BF), jax.ShapeDtypeStruct((t, nq), BF),
                   jax.ShapeDtypeStruct((t, nv), BF)],
        compiler_params=_cparams(("parallel",)),
        name="proj_qkv",
    )(x2d, wq, wqr, wk, wkr, wv, cq, sq, ck, sk)


def _proj_dil_kernel(x_ref, wq_ref, wqr_ref, wk_ref, wkr_ref, wv_ref, cq_ref, sq_ref, ck_ref, sk_ref,
                     *refs):
    outs, scr = refs[:-1], refs[-1]
    n_slabs = scr.shape[0]
    xb = x_ref[...].astype(BF)

    def rope(h, hr, cos, sin):
        return jnp.concatenate(
            [h[:, s * LANES:(s + 1) * LANES] * cos + hr[:, s * LANES:(s + 1) * LANES] * sin
             for s in range(n_slabs)], axis=-1)

    def emit(h, which):
        for s in range(n_slabs):
            scr[s] = h[:, s * LANES:(s + 1) * LANES]
        for p in range(len(outs) // 3):
            o_ref = outs[3 * p + which]
            dil, rows, _ = o_ref.shape
            for c in range(dil):
                for s in range(n_slabs):
                    o_ref[c, :, s * LANES:(s + 1) * LANES] = (
                        scr[s, pl.ds(c, rows, stride=dil), :].astype(o_ref.dtype))

    emit(rope(_dot(xb, wq_ref[...]), _dot(xb, wqr_ref[...]), cq_ref[...], sq_ref[...]), 0)
    emit(rope(_dot(xb, wk_ref[...]), _dot(xb, wkr_ref[...]), ck_ref[...], sk_ref[...]), 1)
    emit(_dot(xb, wv_ref[...]), 2)


def _proj_dil(x2d, batch, seq, wq, wqr, wk, wkr, wv, cq, sq, ck, sk):
    t = x2d.shape[0]
    tm = TM_PROJ
    w = wq.shape[1]
    pos_blocks = seq // tm
    full = lambda a: pl.BlockSpec(a.shape, lambda i: (0, 0))
    tab = pl.BlockSpec((tm, LANES), lambda i: (i % pos_blocks, 0))
    out_specs, out_shape = [], []
    for _, dil in DIL_PAIRS:
        for _ in range(3):
            out_specs.append(pl.BlockSpec((None, dil, tm // dil, w),
                                          lambda i: (i // pos_blocks, 0, i % pos_blocks, 0)))
            out_shape.append(jax.ShapeDtypeStruct((batch, dil, seq // dil, w), BF))
    outs = pl.pallas_call(
        _proj_dil_kernel,
        grid=(t // tm,),
        in_specs=[pl.BlockSpec((tm, D_MODEL), lambda i: (i, 0)),
                  full(wq), full(wqr), full(wk), full(wkr), full(wv), tab, tab, tab, tab],
        out_specs=out_specs,
        out_shape=out_shape,
        scratch_shapes=[pltpu.VMEM((w // LANES, tm, LANES), F32)],
        compiler_params=_cparams(("parallel",)),
        name="proj_dil",
    )(x2d, wq, wqr, wk, wkr, wv, cq, sq, ck, sk)
    return [outs[3 * p:3 * p + 3] for p in range(len(DIL_PAIRS))]


def _rms(x, g, n):
    return x * lax.rsqrt(jnp.sum(x * x, axis=-1, keepdims=True) / n + LN_EPS) * g


def _proj_mla_kernel(x_ref, wcq_ref, wckv_ref, wkr_ref, wkrr_ref, gq_ref, gkv_ref,
                     wuq_ref, wuqr_ref, wuk_ref, wuv_ref, cq_ref, sq_ref, ck_ref, sk_ref,
                     q_ref, k_ref, v_ref):
    xb = x_ref[...].astype(BF)
    nq = _rms(_dot(xb, wcq_ref[...]), gq_ref[...], MLA_Q_RANK).astype(BF)
    nkv = _rms(_dot(xb, wckv_ref[...]), gkv_ref[...], MLA_KV_RANK).astype(BF)
    _rope_store(q_ref, _dot(nq, wuq_ref[...]), _dot(nq, wuqr_ref[...]), cq_ref[...], sq_ref[...])
    kpe = _dot(xb, wkr_ref[...]) * ck_ref[...] + _dot(xb, wkrr_ref[...]) * sk_ref[...]
    kn = _dot(nkv, wuk_ref[...])
    for s in range(MLA_HEADS):
        sl = slice(s * LANES, (s + 1) * LANES)
        k_ref[:, sl] = (kn[:, sl] + kpe).astype(k_ref.dtype)
    v_ref[...] = _dot(nkv, wuv_ref[...]).astype(v_ref.dtype)


def _proj_mla(x2d, seq, wcq, wckv, wkr, wkrr, gq, gkv, wuq, wuqr, wuk, wuv, cq, sq, ck, sk):
    t = x2d.shape[0]
    tm = TM_PROJ
    n = MLA_HEADS * LANES
    pos_blocks = seq // tm
    full = lambda a: pl.BlockSpec(a.shape, lambda i: (0, 0))
    tab = pl.BlockSpec((tm, LANES), lambda i: (i % pos_blocks, 0))
    out = pl.BlockSpec((tm, n), lambda i: (i, 0))
    return pl.pallas_call(
        _proj_mla_kernel,
        grid=(t // tm,),
        in_specs=[pl.BlockSpec((tm, D_MODEL), lambda i: (i, 0)),
                  full(wcq), full(wckv), full(wkr), full(wkrr), full(gq), full(gkv),
                  full(wuq), full(wuqr), full(wuk), full(wuv), tab, tab, tab, tab],
        out_specs=[out, out, out],
        out_shape=[jax.ShapeDtypeStruct((t, n), BF)] * 3,
        compiler_params=_cparams(("parallel",)),
        name="proj_mla",
    )(x2d, wcq, wckv, wkr, wkrr, gq, gkv, wuq, wuqr, wuk, wuv, cq, sq, ck, sk)


def _softmax_pv(q, k, v):
    s = _dot_nt(q, k)
    m = jnp.max(s, axis=-1, keepdims=True)
    p = jnp.exp2(s - m)
    l = jnp.sum(p, axis=-1, keepdims=True)
    return _dot(p.astype(BF), v) / l


def _diff_attn_kernel(lam_ref, g_ref, q1_ref, q2_ref, k1_ref, k2_ref, v_ref, o_ref, *, lambda_init):
    lam = lam_ref[...]
    lam_full = (jnp.exp(jnp.sum(lam[0:1] * lam[1:2], axis=-1, keepdims=True))
                - jnp.exp(jnp.sum(lam[2:3] * lam[3:4], axis=-1, keepdims=True)) + lambda_init)
    v = v_ref[0]
    o1 = _softmax_pv(q1_ref[0], k1_ref[0], v)
    o2 = _softmax_pv(q2_ref[0], k2_ref[0], v)
    o = o1 - lam_full * o2
    o_ref[0] = (_rms(o, g_ref[...], DIFF_V_DIM) * (1.0 - lambda_init)).astype(o_ref.dtype)


def _diff_attn(q, k, v, lam, g_pad, lambda_init):
    b, s, _ = q.shape
    tq = TQ_DENSE
    qspec = lambda off: pl.BlockSpec((1, tq, LANES), lambda bi, h, i: (bi, i, 2 * h + off))
    kspec = lambda off: pl.BlockSpec((1, s, LANES), lambda bi, h, i: (bi, 0, 2 * h + off))
    return pl.pallas_call(
        functools.partial(_diff_attn_kernel, lambda_init=lambda_init),
        grid=(b, DIFF_HEADS, s // tq),
        in_specs=[pl.BlockSpec(lam.shape, lambda bi, h, i: (0, 0)),
                  pl.BlockSpec(g_pad.shape, lambda bi, h, i: (0, 0)),
                  qspec(0), qspec(1), kspec(0), kspec(1),
                  pl.BlockSpec((1, s, LANES), lambda bi, h, i: (bi, 0, h))],
        out_specs=pl.BlockSpec((1, tq, LANES), lambda bi, h, i: (bi, i, h)),
        out_shape=jax.ShapeDtypeStruct((b, s, DIFF_HEADS * LANES), BF),
        compiler_params=_cparams(("parallel", "parallel", "arbitrary")),
        name="diff_attn",
    )(lam, g_pad, q, q, k, k, v)


def _mla_attn_kernel(q_ref, k_ref, v_ref, o_ref):
    o_ref[0] = _softmax_pv(q_ref[0], k_ref[0], v_ref[0]).astype(o_ref.dtype)


def _mla_attn(q, k, v):
    b, s, _ = q.shape
    tq = TQ_DENSE
    kv = pl.BlockSpec((1, s, LANES), lambda bi, h, i: (bi, 0, h))
    qo = pl.BlockSpec((1, tq, LANES), lambda bi, h, i: (bi, i, h))
    return pl.pallas_call(
        _mla_attn_kernel,
        grid=(b, MLA_HEADS, s // tq),
        in_specs=[qo, kv, kv],
        out_specs=qo,
        out_shape=jax.ShapeDtypeStruct((b, s, MLA_HEADS * LANES), BF),
        compiler_params=_cparams(("parallel", "parallel", "arbitrary")),
        name="mla_attn",
    )(q, k, v)


def _band_attn_kernel(q_ref, kp_ref, kc_ref, kn_ref, vp_ref, vc_ref, vn_ref, o_ref, lse_ref, *, seg_len):
    i = pl.program_id(2)
    tq = q_ref.shape[0]
    blk = LANES
    span = blk + 2 * BAND_HALF
    kk = jnp.concatenate([kp_ref[...], kc_ref[...], kn_ref[...]], axis=0)
    vv = jnp.concatenate([vp_ref[...], vc_ref[...], vn_ref[...]], axis=0)
    qi = lax.broadcasted_iota(jnp.int32, (blk, span), 0)
    kj = lax.broadcasted_iota(jnp.int32, (blk, span), 1)
    for j in range(tq // blk):
        qpos = i * tq + j * blk + qi
        kpos = i * tq + j * blk - BAND_HALF + kj
        mask = (jnp.abs(kpos - qpos) <= BAND_HALF) & (kpos >= 0) & (kpos < seg_len)
        for h in range(DIL_HEADS):
            cs = slice(h * LANES, (h + 1) * LANES)
            rows = slice(j * blk, (j + 1) * blk)
            k = kk[j * blk:j * blk + span, cs]
            v = vv[j * blk:j * blk + span, cs]
            s = jnp.where(mask, _dot_nt(q_ref[rows, cs], k), NEG_INF)
            m = jnp.max(s, axis=-1, keepdims=True)
            p = jnp.exp2(s - m)
            l = jnp.sum(p, axis=-1, keepdims=True)
            o_ref[rows, cs] = _dot(p.astype(BF), v) / l
            lse_ref[rows, cs] = jnp.broadcast_to(m * LN2 + jnp.log(l), (blk, LANES))


def _band_attn(q, k, v):
    b, dil, seg, w = q.shape
    tq = TQ_BAND
    r = tq // BAND_HALF
    nb = seg // BAND_HALF
    cur = pl.BlockSpec((None, None, tq, w), lambda bi, c, i: (bi, c, i, 0))
    prev = pl.BlockSpec((None, None, BAND_HALF, w), lambda bi, c, i: (bi, c, jnp.maximum(i * r - 1, 0), 0))
    nxt = pl.BlockSpec((None, None, BAND_HALF, w),
                       lambda bi, c, i: (bi, c, jnp.minimum((i + 1) * r, nb - 1), 0))
    return pl.pallas_call(
        functools.partial(_band_attn_kernel, seg_len=seg),
        grid=(b, dil, seg // tq),
        in_specs=[cur, prev, cur, nxt, prev, cur, nxt],
        out_specs=[cur, cur],
        out_shape=[jax.ShapeDtypeStruct((b, dil, seg, w), F32)] * 2,
        compiler_params=_cparams(("parallel", "parallel", "arbitrary")),
        name="band_attn",
    )(q, k, k, k, v, v, v)


def _layer_norm(y, g, b):
    mu = jnp.mean(y, axis=-1, keepdims=True)
    yc = y - mu
    var = jnp.mean(yc * yc, axis=-1, keepdims=True)
    return yc * lax.rsqrt(var + LN_EPS) * g + b


def _to_token_order(src_ref, dst_ref):
    dil, rows, w = src_ref.shape
    for c in range(dil):
        for s in range(w // LANES):
            dst_ref[s, pl.ds(c, rows, stride=dil), :] = src_ref[c, :, s * LANES:(s + 1) * LANES]


def _out_proj_kernel(x_ref, oa_ref, ob_ref, o1_ref, o2_ref, o3_ref, l1_ref, l2_ref, l3_ref,
                     wa_ref, wb_ref, wc_ref, g_ref, b_ref, y_ref, o2_scr, o3_scr, l2_scr, l3_scr):
    _to_token_order(o2_ref, o2_scr)
    _to_token_order(o3_ref, o3_scr)
    _to_token_order(l2_ref, l2_scr)
    _to_token_order(l3_ref, l3_scr)
    heads = []
    for s in range(DIL_HEADS):
        sl = slice(s * LANES, (s + 1) * LANES)
        l1, l2, l3 = l1_ref[0, :, sl], l2_scr[s], l3_scr[s]
        m = jnp.maximum(jnp.maximum(l1, l2), l3)
        e1, e2, e3 = jnp.exp(l1 - m), jnp.exp(l2 - m), jnp.exp(l3 - m)
        oc = (e1 * o1_ref[0, :, sl] + e2 * o2_scr[s] + e3 * o3_scr[s]) / (e1 + e2 + e3)
        heads.append(oc.astype(BF))
    mix = (_dot(oa_ref[...], wa_ref[...]) + _dot(ob_ref[...], wb_ref[...])
           + _dot(jnp.concatenate(heads, axis=-1), wc_ref[...]))
    y_ref[...] = _layer_norm(DEEPNORM_ALPHA * x_ref[...] + mix, g_ref[...], b_ref[...])


def _out_proj(x2d, seq, oa, ob, band, wa, wb, wc, g, bb):
    t = x2d.shape[0]
    tm = TM_PROJ
    pos_blocks = seq // tm
    w = DIL_HEADS * LANES
    row = lambda a: pl.BlockSpec((tm, a.shape[1]), lambda i: (i, 0))
    full = lambda a: pl.BlockSpec(a.shape, lambda i: (0, 0))
    cls = lambda a: pl.BlockSpec((None, a.shape[1], tm // a.shape[1], w),
                                 lambda i: (i // pos_blocks, 0, i % pos_blocks, 0))
    (o1, l1), (o2, l2), (o3, l3) = band
    slab = pltpu.VMEM((DIL_HEADS, tm, LANES), F32)
    return pl.pallas_call(
        _out_proj_kernel,
        grid=(t // tm,),
        in_specs=[row(x2d), row(oa), row(ob), cls(o1), cls(o2), cls(o3), cls(l1), cls(l2), cls(l3),
                  full(wa), full(wb), full(wc), full(g), full(bb)],
        out_specs=row(x2d),
        out_shape=jax.ShapeDtypeStruct((t, D_MODEL), F32),
        scratch_shapes=[slab, slab, slab, slab],
        compiler_params=_cparams(("parallel",)),
        name="out_proj_ln",
    )(x2d, oa, ob, o1, o2, o3, l1, l2, l3, wa, wb, wc, g, bb)


def _extract16(s, want_rank):
    row = lax.broadcasted_iota(jnp.int32, (PEER_TOPK, s.shape[1]), 0)
    work, top = s, jnp.zeros((PEER_TOPK, s.shape[1]), F32)
    rank = jnp.full(s.shape, float(PEER_TOPK), F32)
    for kk in range(PEER_TOPK):
        m = jnp.max(work, axis=0, keepdims=True)
        sel = work == m
        work = jnp.where(sel, -jnp.inf, work)
        top = jnp.where(row == kk, m, top)
        if want_rank:
            rank = jnp.where(sel, float(kk), rank)
    return top, rank


def _route_chunk(s1, s2):
    t1, _ = _extract16(s1, False)
    t2, rank2 = _extract16(s2, True)
    cand = jnp.concatenate(
        [t1[0:1] + t2]
        + [t1[k1:k1 + 1] + t2[0:8] for k1 in range(1, 8)]
        + [t1[8:16] + t2[0:1]], axis=0)
    c0 = t1[0:1] + t2[0:1]
    work, z, tau = cand, jnp.zeros_like(c0), c0
    for _ in range(PEER_TOPK):
        tau = jnp.max(work, axis=0, keepdims=True)
        work = jnp.where(work == tau, -jnp.inf, work)
        z = z + jnp.exp(tau - c0)
    cnt_of_rank = jnp.zeros_like(t1)
    for k2 in range(PEER_TOPK):
        cnt_of_rank = cnt_of_rank + jnp.where(t1 + t2[k2:k2 + 1] >= tau, 1.0, 0.0)
    cnt = jnp.zeros_like(s1)
    for k1 in range(PEER_TOPK):
        cnt = jnp.where(s1 == t1[k1:k1 + 1], cnt_of_rank[k1:k1 + 1], cnt)
    e1 = jnp.exp(s1 - t1[0:1]) / z
    e2 = jnp.exp(s2 - t2[0:1])
    return e1, cnt, e2, rank2


def _peer_route_kernel(x_ref, wq_ref, keys_ref, e1_ref, cnt_ref, e2_ref, rk2_ref, q_scr, s_scr):
    nk = PEER_N_KEYS
    n_chunks = e1_ref.shape[0]
    q = _dot(x_ref[...].astype(BF), wq_ref[...]).astype(BF)
    for s in range(2 * PEER_HEADS):
        q_scr[s] = q[:, s * nk:(s + 1) * nk]

    def head_body(h, carry):
        s1 = _dot_nt(keys_ref[2 * h], q_scr[2 * h])
        s2 = _dot_nt(keys_ref[2 * h + 1], q_scr[2 * h + 1])
        for c in range(n_chunks):
            s_scr[0, c] = s1[:, c * LANES:(c + 1) * LANES]
            s_scr[1, c] = s2[:, c * LANES:(c + 1) * LANES]
        rows = pl.ds(pl.multiple_of(h * nk, nk), nk)

        def chunk_body(c, carry2):
            e1, cnt, e2, rank2 = _route_chunk(s_scr[0, c], s_scr[1, c])
            e1_ref[c, rows, :] = e1
            cnt_ref[c, rows, :] = cnt
            e2_ref[c, rows, :] = e2.astype(e2_ref.dtype)
            rk2_ref[c, rows, :] = rank2.astype(rk2_ref.dtype)
            return carry2

        lax.fori_loop(0, n_chunks, chunk_body, 0)
        return carry

    lax.fori_loop(0, PEER_HEADS, head_body, 0)


def _peer_route(x2d, wq, keys):
    t = x2d.shape[0]
    tt = TT_PEER
    nc = tt // LANES
    rows = PEER_HEADS * PEER_N_KEYS
    out = pl.BlockSpec((nc, rows, LANES), lambda i: (i, 0, 0))
    shape = lambda dt: jax.ShapeDtypeStruct((t // LANES, rows, LANES), dt)
    return pl.pallas_call(
        _peer_route_kernel,
        grid=(t // tt,),
        in_specs=[pl.BlockSpec((tt, D_MODEL), lambda i: (i, 0)),
                  pl.BlockSpec(wq.shape, lambda i: (0, 0)),
                  pl.BlockSpec(keys.shape, lambda i: (0, 0, 0))],
        out_specs=[out] * 4,
        out_shape=[shape(F32)] * 4,
        scratch_shapes=[pltpu.VMEM((2 * PEER_HEADS, tt, PEER_N_KEYS), BF),
                        pltpu.VMEM((2, nc, PEER_N_KEYS, LANES), F32)],
        compiler_params=_cparams(("parallel",)),
        name="peer_route",
    )(x2d, wq, keys)


def _gelu(x):
    return 0.5 * x * (1.0 + lax.erf(x * (2.0 ** -0.5)))


def _peer_expert_kernel(x_ref, e1_ref, cnt_ref, e2_ref, rk2_ref, u_ref, vt_ref, g_ref, b_ref,
                        y_ref, acc_ref, a_ref, xb_ref):
    j = pl.program_id(1)
    nk = PEER_N_KEYS
    rows_per_step = u_ref.shape[0] // nk
    n_chunks = e1_ref.shape[0]

    @pl.when(j == 0)
    def _():
        acc_ref[...] = jnp.zeros_like(acc_ref)
        xb_ref[...] = x_ref[...].astype(BF)

    ht = _dot_nt(u_ref[...], xb_ref[...])
    for r in range(rows_per_step):
        a = j * rows_per_step + r
        rs = slice(r * nk, (r + 1) * nk)
        for c in range(n_chunks):
            cs = slice(c * LANES, (c + 1) * LANES)
            w = jnp.zeros((nk, LANES), F32)
            for h in range(PEER_HEADS):
                e1 = e1_ref[c, pl.ds(h * nk + a, 1), :]
                cnt = cnt_ref[c, pl.ds(h * nk + a, 1), :]
                hs = slice(h * nk, (h + 1) * nk)
                w = w + jnp.where(rk2_ref[c, hs, :] < cnt, e1 * e2_ref[c, hs, :], 0.0)
            a_ref[rs, cs] = (w * _gelu(ht[rs, cs])).astype(BF)
    acc_ref[...] += _dot(vt_ref[...], a_ref[...])

    @pl.when(j == pl.num_programs(1) - 1)
    def _():
        ffn = acc_ref[...].T
        y_ref[...] = _layer_norm(DEEPNORM_ALPHA * x_ref[...] + ffn, g_ref[...], b_ref[...])


def _peer_expert(x2d, e1, cnt, e2, rk2, u, vt, g, bb):
    t = x2d.shape[0]
    tt, eb = TT_PEER, EB_PEER
    nc = tt // LANES
    rows = PEER_HEADS * PEER_N_KEYS
    route = pl.BlockSpec((nc, rows, LANES), lambda i, j: (i, 0, 0))
    vec = pl.BlockSpec((1, D_MODEL), lambda i, j: (0, 0))
    return pl.pallas_call(
        _peer_expert_kernel,
        grid=(t // tt, PEER_N_EXPERTS // eb),
        in_specs=[pl.BlockSpec((tt, D_MODEL), lambda i, j: (i, 0)),
                  route, route, route, route,
                  pl.BlockSpec((eb, D_MODEL), lambda i, j: (j, 0)),
                  pl.BlockSpec((D_MODEL, eb), lambda i, j: (0, j)),
                  vec, vec],
        out_specs=pl.BlockSpec((tt, D_MODEL), lambda i, j: (i, 0)),
        out_shape=jax.ShapeDtypeStruct((t, D_MODEL), F32),
        scratch_shapes=[pltpu.VMEM((D_MODEL, tt), F32), pltpu.VMEM((eb, tt), BF),
                        pltpu.VMEM((tt, D_MODEL), BF)],
        compiler_params=_cparams(("parallel", "arbitrary")),
        name="peer_expert_ln",
    )(x2d, e1, cnt, e2, rk2, u, vt, g, bb)


def _split_w_in(w):
    sizes = (256, 256, 256, MLA_Q_RANK, MLA_KV_RANK, MLA_ROPE_DIM, 384, 384, 384)
    out, off = [], 0
    for n in sizes:
        out.append(w[:, off:off + n])
        off += n
    return out


def _pad_rows(w, n_heads, d):
    n = w.shape[1]
    w3 = w.reshape(n_heads, d, n)
    return jnp.zeros((n_heads, LANES, n), w.dtype).at[:, :d, :].set(w3).reshape(n_heads * LANES, n)


def _layer(x2d, batch, seq, l, w_in, diff_lambda, diff_subln_g, mla_q_norm_g, mla_kv_norm_g, mla_w_uq,
           mla_w_ukv, w_o, ln1_g, ln1_b, peer_w_q, peer_sub_keys, peer_u, peer_v, ln2_g, ln2_b):
    lambda_init = 0.8 - 0.6 * math.exp(-0.3 * (l + 1))
    a_q, a_k, a_v, b_cq, b_ckv, b_kr, c_q, c_k, c_v = _split_w_in(w_in)
    bf = lambda a: a.astype(BF)

    cq, sq = _rope_tables(seq, DIFF_QK_DIM, 0, LOG2E * DIFF_QK_DIM ** -0.5, 0.0)
    ck, sk = _rope_tables(seq, DIFF_QK_DIM, 0, 1.0, 0.0)
    nh = 2 * DIFF_HEADS
    qa, ka, va = _proj_qkv(
        x2d, seq,
        bf(_pad_heads(a_q, nh, DIFF_QK_DIM)), bf(_pad_heads(_rot_cols(a_q, nh, DIFF_QK_DIM), nh, DIFF_QK_DIM)),
        bf(_pad_heads(a_k, nh, DIFF_QK_DIM)), bf(_pad_heads(_rot_cols(a_k, nh, DIFF_QK_DIM), nh, DIFF_QK_DIM)),
        bf(_pad_heads(a_v, DIFF_HEADS, DIFF_V_DIM)), cq, sq, ck, sk)
    g_pad = jnp.zeros((1, LANES), F32).at[0, :DIFF_V_DIM].set(diff_subln_g.astype(F32))
    sh = lambda a: a.reshape(batch, seq, a.shape[-1])
    out_a = _diff_attn(sh(qa), sh(ka), sh(va), diff_lambda.astype(F32), g_pad, lambda_init)

    qk_dim = MLA_NOPE_DIM + MLA_ROPE_DIM
    cqm, sqm = _rope_tables(seq, MLA_ROPE_DIM, MLA_NOPE_DIM, LOG2E * qk_dim ** -0.5, 1.0)
    cqm = cqm.at[:, qk_dim:].set(0.0)
    ckm, skm = _rope_tables(seq, MLA_ROPE_DIM, MLA_NOPE_DIM, 1.0, 0.0)
    uq3 = mla_w_uq.reshape(MLA_Q_RANK, MLA_HEADS, qk_dim)
    uq_rope = uq3[..., MLA_NOPE_DIM:].reshape(MLA_Q_RANK, MLA_HEADS * MLA_ROPE_DIM)
    uq_rot = _pad_heads(_rot_cols(uq_rope, MLA_HEADS, MLA_ROPE_DIM), MLA_HEADS, MLA_ROPE_DIM, MLA_NOPE_DIM)
    ukv3 = mla_w_ukv.reshape(MLA_KV_RANK, MLA_HEADS, MLA_NOPE_DIM + MLA_V_DIM)
    uk = ukv3[..., :MLA_NOPE_DIM].reshape(MLA_KV_RANK, MLA_HEADS * MLA_NOPE_DIM)
    uv = ukv3[..., MLA_NOPE_DIM:].reshape(MLA_KV_RANK, MLA_HEADS * MLA_V_DIM)
    qb, kb, vb = _proj_mla(
        x2d, seq, bf(b_cq), bf(b_ckv),
        bf(_pad_heads(b_kr, 1, MLA_ROPE_DIM, MLA_NOPE_DIM)),
        bf(_pad_heads(_rot_cols(b_kr, 1, MLA_ROPE_DIM), 1, MLA_ROPE_DIM, MLA_NOPE_DIM)),
        mla_q_norm_g.astype(F32)[None, :], mla_kv_norm_g.astype(F32)[None, :],
        bf(_pad_heads(mla_w_uq, MLA_HEADS, qk_dim)), bf(uq_rot),
        bf(_pad_heads(uk, MLA_HEADS, MLA_NOPE_DIM)), bf(_pad_heads(uv, MLA_HEADS, MLA_V_DIM)),
        cqm, sqm, ckm, skm)
    out_b = _mla_attn(sh(qb), sh(kb), sh(vb))

    cqd, sqd = _rope_tables(seq, DIL_HEAD_DIM, 0, LOG2E * DIL_HEAD_DIM ** -0.5, 0.0)
    ckd, skd = _rope_tables(seq, DIL_HEAD_DIM, 0, 1.0, 0.0)
    nh = DIL_HEADS
    dil_qkv = _proj_dil(
        x2d, batch, seq,
        bf(_pad_heads(c_q, nh, DIL_HEAD_DIM)), bf(_pad_heads(_rot_cols(c_q, nh, DIL_HEAD_DIM), nh, DIL_HEAD_DIM)),
        bf(_pad_heads(c_k, nh, DIL_HEAD_DIM)), bf(_pad_heads(_rot_cols(c_k, nh, DIL_HEAD_DIM), nh, DIL_HEAD_DIM)),
        bf(_pad_heads(c_v, nh, DIL_HEAD_DIM)), cqd, sqd, ckd, skd)
    band = [_band_attn(q, k, v) for q, k, v in dil_qkv]
    fl = lambda a: a.reshape(batch * seq, a.shape[-1])

    wo_a = _pad_rows(w_o[:DIFF_HEADS * DIFF_V_DIM], DIFF_HEADS, DIFF_V_DIM)
    wo_b = _pad_rows(w_o[256:256 + MLA_HEADS * MLA_V_DIM], MLA_HEADS, MLA_V_DIM)
    wo_c = _pad_rows(w_o[640:], DIL_HEADS, DIL_HEAD_DIM)
    x1 = _out_proj(x2d, seq, fl(out_a), fl(out_b), band,
                   bf(wo_a), bf(wo_b), bf(wo_c), ln1_g.astype(F32)[None, :], ln1_b.astype(F32)[None, :])

    keys = bf(peer_sub_keys.reshape(PEER_HEADS * 2, PEER_N_KEYS, PEER_HALF_DIM))
    e1, cnt, e2, rk2 = _peer_route(x1, bf(peer_w_q), keys)
    return _peer_expert(x1, e1, cnt, e2, rk2, bf(peer_u), bf(peer_v).T,
                        ln2_g.astype(F32)[None, :], ln2_b.astype(F32)[None, :])


def kernel(x, w_in, diff_lambda, diff_subln_g, mla_q_norm_g, mla_kv_norm_g, mla_w_uq, mla_w_ukv, w_o,
           ln1_g, ln1_b, peer_w_q, peer_sub_keys, peer_u, peer_v, ln2_g, ln2_b):
    batch, seq, d = x.shape
    params = (w_in, diff_lambda, diff_subln_g, mla_q_norm_g, mla_kv_norm_g, mla_w_uq, mla_w_ukv, w_o,
              ln1_g, ln1_b, peer_w_q, peer_sub_keys, peer_u, peer_v, ln2_g, ln2_b)
    x2d = x.reshape(batch * seq, d)
    for l in range(DEPTH):
        x2d = _layer(x2d, batch, seq, l, *[p[l] for p in params])
    return x2d.reshape(batch, seq, d)
```

```python
import functools
import math

import jax
import jax.numpy as jnp
from jax import lax
from jax.experimental import pallas as pl
from jax.experimental.pallas import tpu as pltpu

BF = jnp.bfloat16
F32 = jnp.float32

LANES = 128
SUBLANES = 8
D_MODEL = 1024
DEPTH = 2
DIFF_HEADS = 4
DIFF_QK_DIM = 32
DIFF_V_DIM = 64
MLA_HEADS = 6
MLA_Q_RANK = 256
MLA_KV_RANK = 128
MLA_NOPE_DIM = 64
MLA_ROPE_DIM = 32
MLA_V_DIM = 64
DIL_HEADS = 6
DIL_HEAD_DIM = 64
DIL_PAIRS = ((128, 1), (512, 4), (2048, 16))
ROPE_THETA = 10000.0
PEER_HEADS = 8
PEER_N_KEYS = 128
PEER_N_EXPERTS = PEER_N_KEYS * PEER_N_KEYS
PEER_HALF_DIM = 128
PEER_TOPK = 16
DEEPNORM_ALPHA = (2 * DEPTH) ** 0.25
LN_EPS = 1e-5
NEG_INF = -1e30
LOG2E = math.log2(math.e)
LN2 = math.log(2.0)

VMEM_LIMIT = 56 * 1024 * 1024

TM_PROJ = 512
TQ_DENSE = 256
TQ_BAND = 256
BAND_HALF = 64
TT_PEER = 512
EB_PEER = 1024
ROW_GROUP = 2


def _cparams(sem):
    return pltpu.CompilerParams(dimension_semantics=sem, vmem_limit_bytes=VMEM_LIMIT)


def _dot(a, b):
    return jnp.dot(a, b, preferred_element_type=F32)


def _dot_nt(a, b):
    return lax.dot_general(a, b, (((1,), (1,)), ((), ())), preferred_element_type=F32)


def _pad_heads(w, n_heads, d, offset=0):
    k = w.shape[0]
    w3 = w.reshape(k, n_heads, d)
    out = jnp.zeros((k, n_heads, LANES), w.dtype).at[:, :, offset:offset + d].set(w3)
    return out.reshape(k, n_heads * LANES)


def _rot_cols(w, n_heads, d):
    k = w.shape[0]
    w3 = w.reshape(k, n_heads, d)
    half = d // 2
    return jnp.concatenate([-w3[..., half:], w3[..., :half]], axis=-1).reshape(k, n_heads * d)


def _rope_tables(seq, d, offset, scale, fill):
    half = d // 2
    inv_freq = jnp.exp(-math.log(ROPE_THETA) * jnp.arange(half, dtype=F32) / half)
    ang = jnp.arange(seq, dtype=jnp.int32).astype(F32)[:, None] * inv_freq[None, :]
    cos, sin = jnp.cos(ang), jnp.sin(ang)
    cos_t = jnp.full((seq, LANES), fill, F32).at[:, offset:offset + d].set(jnp.concatenate([cos, cos], -1))
    sin_t = jnp.zeros((seq, LANES), F32).at[:, offset:offset + d].set(jnp.concatenate([sin, sin], -1))
    return cos_t * scale, sin_t * scale


def _rope_store(o_ref, h, hr, cos, sin):
    for s in range(o_ref.shape[1] // LANES):
        sl = slice(s * LANES, (s + 1) * LANES)
        o_ref[:, sl] = (h[:, sl] * cos + hr[:, sl] * sin).astype(o_ref.dtype)


def _proj_qkv_kernel(x_ref, wq_ref, wqr_ref, wk_ref, wkr_ref, wv_ref, cq_ref, sq_ref, ck_ref, sk_ref,
                     q_ref, k_ref, v_ref):
    xb = x_ref[...].astype(BF)
    _rope_store(q_ref, _dot(xb, wq_ref[...]), _dot(xb, wqr_ref[...]), cq_ref[...], sq_ref[...])
    _rope_store(k_ref, _dot(xb, wk_ref[...]), _dot(xb, wkr_ref[...]), ck_ref[...], sk_ref[...])
    v_ref[...] = _dot(xb, wv_ref[...]).astype(v_ref.dtype)


def _proj_qkv(x2d, seq, wq, wqr, wk, wkr, wv, cq, sq, ck, sk):
    t = x2d.shape[0]
    tm = TM_PROJ
    nq, nv = wq.shape[1], wv.shape[1]
    pos_blocks = seq // tm
    full = lambda a: pl.BlockSpec(a.shape, lambda i: (0, 0))
    tab = pl.BlockSpec((tm, LANES), lambda i: (i % pos_blocks, 0))
    return pl.pallas_call(
        _proj_qkv_kernel,
        grid=(t // tm,),
        in_specs=[pl.BlockSpec((tm, D_MODEL), lambda i: (i, 0)),
                  full(wq), full(wqr), full(wk), full(wkr), full(wv), tab, tab, tab, tab],
        out_specs=[pl.BlockSpec((tm, nq), lambda i: (i, 0)),
                   pl.BlockSpec((tm, nq), lambda i: (i, 0)),
                   pl.BlockSpec((tm, nv), lambda i: (i, 0))],
        out_shape=[jax.ShapeDtypeStruct((t, nq), BF), jax.ShapeDtypeStruct((t, nq), BF),
                   jax.ShapeDtypeStruct((t, nv), BF)],
        compiler_params=_cparams(("parallel",)),
        name="proj_qkv",
    )(x2d, wq, wqr, wk, wkr, wv, cq, sq, ck, sk)


def _proj_dil_kernel(x_ref, wq_ref, wqr_ref, wk_ref, wkr_ref, wv_ref, cq_ref, sq_ref, ck_ref, sk_ref,
                     *refs):
    outs, scr = refs[:-1], refs[-1]
    n_slabs = scr.shape[0]
    xb = x_ref[...].astype(BF)

    def rope(h, hr, cos, sin):
        return jnp.concatenate(
            [h[:, s * LANES:(s + 1) * LANES] * cos + hr[:, s * LANES:(s + 1) * LANES] * sin
             for s in range(n_slabs)], axis=-1)

    def emit(h, which):
        for s in range(n_slabs):
            scr[s] = h[:, s * LANES:(s + 1) * LANES]
        for p in range(len(outs) // 3):
            o_ref = outs[3 * p + which]
            dil, rows, _ = o_ref.shape
            for c in range(dil):
                for s in range(n_slabs):
                    o_ref[c, :, s * LANES:(s + 1) * LANES] = (
                        scr[s, pl.ds(c, rows, stride=dil), :].astype(o_ref.dtype))

    emit(rope(_dot(xb, wq_ref[...]), _dot(xb, wqr_ref[...]), cq_ref[...], sq_ref[...]), 0)
    emit(rope(_dot(xb, wk_ref[...]), _dot(xb, wkr_ref[...]), ck_ref[...], sk_ref[...]), 1)
    emit(_dot(xb, wv_ref[...]), 2)


def _proj_dil(x2d, batch, seq, wq, wqr, wk, wkr, wv, cq, sq, ck, sk):
    t = x2d.shape[0]
    tm = TM_PROJ
    w = wq.shape[1]
    pos_blocks = seq // tm
    full = lambda a: pl.BlockSpec(a.shape, lambda i: (0, 0))
    tab = pl.BlockSpec((tm, LANES), lambda i: (i % pos_blocks, 0))
    out_specs, out_shape = [], []
    for _, dil in DIL_PAIRS:
        for _ in range(3):
            out_specs.append(pl.BlockSpec((None, dil, tm // dil, w),
                                          lambda i: (i // pos_blocks, 0, i % pos_blocks, 0)))
            out_shape.append(jax.ShapeDtypeStruct((batch, dil, seq // dil, w), BF))
    outs = pl.pallas_call(
        _proj_dil_kernel,
        grid=(t // tm,),
        in_specs=[pl.BlockSpec((tm, D_MODEL), lambda i: (i, 0)),
                  full(wq), full(wqr), full(wk), full(wkr), full(wv), tab, tab, tab, tab],
        out_specs=out_specs,
        out_shape=out_shape,
        scratch_shapes=[pltpu.VMEM((w // LANES, tm, LANES), F32)],
        compiler_params=_cparams(("parallel",)),
        name="proj_dil",
    )(x2d, wq, wqr, wk, wkr, wv, cq, sq, ck, sk)
    return [outs[3 * p:3 * p + 3] for p in range(len(DIL_PAIRS))]


def _rms(x, g, n):
    return x * lax.rsqrt(jnp.sum(x * x, axis=-1, keepdims=True) / n + LN_EPS) * g


def _proj_mla_kernel(x_ref, wcq_ref, wckv_ref, wkr_ref, wkrr_ref, gq_ref, gkv_ref,
                     wuq_ref, wuqr_ref, wuk_ref, wuv_ref, cq_ref, sq_ref, ck_ref, sk_ref,
                     q_ref, k_ref, v_ref):
    xb = x_ref[...].astype(BF)
    nq = _rms(_dot(xb, wcq_ref[...]), gq_ref[...], MLA_Q_RANK).astype(BF)
    nkv = _rms(_dot(xb, wckv_ref[...]), gkv_ref[...], MLA_KV_RANK).astype(BF)
    _rope_store(q_ref, _dot(nq, wuq_ref[...]), _dot(nq, wuqr_ref[...]), cq_ref[...], sq_ref[...])
    kpe = _dot(xb, wkr_ref[...]) * ck_ref[...] + _dot(xb, wkrr_ref[...]) * sk_ref[...]
    kn = _dot(nkv, wuk_ref[...])
    for s in range(MLA_HEADS):
        sl = slice(s * LANES, (s + 1) * LANES)
        k_ref[:, sl] = (kn[:, sl] + kpe).astype(k_ref.dtype)
    v_ref[...] = _dot(nkv, wuv_ref[...]).astype(v_ref.dtype)


def _proj_mla(x2d, seq, wcq, wckv, wkr, wkrr, gq, gkv, wuq, wuqr, wuk, wuv, cq, sq, ck, sk):
    t = x2d.shape[0]
    tm = TM_PROJ
    n = MLA_HEADS * LANES
    pos_blocks = seq // tm
    full = lambda a: pl.BlockSpec(a.shape, lambda i: (0, 0))
    tab = pl.BlockSpec((tm, LANES), lambda i: (i % pos_blocks, 0))
    out = pl.BlockSpec((tm, n), lambda i: (i, 0))
    return pl.pallas_call(
        _proj_mla_kernel,
        grid=(t // tm,),
        in_specs=[pl.BlockSpec((tm, D_MODEL), lambda i: (i, 0)),
                  full(wcq), full(wckv), full(wkr), full(wkrr), full(gq), full(gkv),
                  full(wuq), full(wuqr), full(wuk), full(wuv), tab, tab, tab, tab],
        out_specs=[out, out, out],
        out_shape=[jax.ShapeDtypeStruct((t, n), BF)] * 3,
        compiler_params=_cparams(("parallel",)),
        name="proj_mla",
    )(x2d, wcq, wckv, wkr, wkrr, gq, gkv, wuq, wuqr, wuk, wuv, cq, sq, ck, sk)


def _softmax_pv(q, k, v):
    s = _dot_nt(q, k)
    m = jnp.max(s, axis=-1, keepdims=True)
    p = jnp.exp2(s - m)
    l = jnp.sum(p, axis=-1, keepdims=True)
    return _dot(p.astype(BF), v) / l


def _diff_attn_kernel(lam_ref, g_ref, q1_ref, q2_ref, k1_ref, k2_ref, v_ref, o_ref, *, lambda_init):
    lam = lam_ref[...]
    lam_full = (jnp.exp(jnp.sum(lam[0:1] * lam[1:2], axis=-1, keepdims=True))
                - jnp.exp(jnp.sum(lam[2:3] * lam[3:4], axis=-1, keepdims=True)) + lambda_init)
    v = v_ref[0]
    o1 = _softmax_pv(q1_ref[0], k1_ref[0], v)
    o2 = _softmax_pv(q2_ref[0], k2_ref[0], v)
    o = o1 - lam_full * o2
    o_ref[0] = (_rms(o, g_ref[...], DIFF_V_DIM) * (1.0 - lambda_init)).astype(o_ref.dtype)


def _diff_attn(q, k, v, lam, g_pad, lambda_init):
    b, s, _ = q.shape
    tq = TQ_DENSE
    qspec = lambda off: pl.BlockSpec((1, tq, LANES), lambda bi, h, i: (bi, i, 2 * h + off))
    kspec = lambda off: pl.BlockSpec((1, s, LANES), lambda bi, h, i: (bi, 0, 2 * h + off))
    return pl.pallas_call(
        functools.partial(_diff_attn_kernel, lambda_init=lambda_init),
        grid=(b, DIFF_HEADS, s // tq),
        in_specs=[pl.BlockSpec(lam.shape, lambda bi, h, i: (0, 0)),
                  pl.BlockSpec(g_pad.shape, lambda bi, h, i: (0, 0)),
                  qspec(0), qspec(1), kspec(0), kspec(1),
                  pl.BlockSpec((1, s, LANES), lambda bi, h, i: (bi, 0, h))],
        out_specs=pl.BlockSpec((1, tq, LANES), lambda bi, h, i: (bi, i, h)),
        out_shape=jax.ShapeDtypeStruct((b, s, DIFF_HEADS * LANES), BF),
        compiler_params=_cparams(("parallel", "parallel", "arbitrary")),
        name="diff_attn",
    )(lam, g_pad, q, q, k, k, v)


def _mla_attn_kernel(q_ref, k_ref, v_ref, o_ref):
    o_ref[0] = _softmax_pv(q_ref[0], k_ref[0], v_ref[0]).astype(o_ref.dtype)


def _mla_attn(q, k, v):
    b, s, _ = q.shape
    tq = TQ_DENSE
    kv = pl.BlockSpec((1, s, LANES), lambda bi, h, i: (bi, 0, h))
    qo = pl.BlockSpec((1, tq, LANES), lambda bi, h, i: (bi, i, h))
    return pl.pallas_call(
        _mla_attn_kernel,
        grid=(b, MLA_HEADS, s // tq),
        in_specs=[qo, kv, kv],
        out_specs=qo,
        out_shape=jax.ShapeDtypeStruct((b, s, MLA_HEADS * LANES), BF),
        compiler_params=_cparams(("parallel", "parallel", "arbitrary")),
        name="mla_attn",
    )(q, k, v)


def _band_attn_kernel(q_ref, kp_ref, kc_ref, kn_ref, vp_ref, vc_ref, vn_ref, o_ref, lse_ref, *, seg_len):
    i = pl.program_id(2)
    tq = q_ref.shape[0]
    blk = LANES
    span = blk + 2 * BAND_HALF
    kk = jnp.concatenate([kp_ref[...], kc_ref[...], kn_ref[...]], axis=0)
    vv = jnp.concatenate([vp_ref[...], vc_ref[...], vn_ref[...]], axis=0)
    qi = lax.broadcasted_iota(jnp.int32, (blk, span), 0)
    kj = lax.broadcasted_iota(jnp.int32, (blk, span), 1)
    for j in range(tq // blk):
        qpos = i * tq + j * blk + qi
        kpos = i * tq + j * blk - BAND_HALF + kj
        mask = (jnp.abs(kpos - qpos) <= BAND_HALF) & (kpos >= 0) & (kpos < seg_len)
        for h in range(DIL_HEADS):
            cs = slice(h * LANES, (h + 1) * LANES)
            rows = slice(j * blk, (j + 1) * blk)
            k = kk[j * blk:j * blk + span, cs]
            v = vv[j * blk:j * blk + span, cs]
            s = jnp.where(mask, _dot_nt(q_ref[rows, cs], k), NEG_INF)
            m = jnp.max(s, axis=-1, keepdims=True)
            p = jnp.exp2(s - m)
            l = jnp.sum(p, axis=-1, keepdims=True)
            o_ref[rows, cs] = _dot(p.astype(BF), v) / l
            lse_ref[rows, cs] = jnp.broadcast_to(m * LN2 + jnp.log(l), (blk, LANES))


def _band_attn(q, k, v):
    b, dil, seg, w = q.shape
    tq = TQ_BAND
    r = tq // BAND_HALF
    nb = seg // BAND_HALF
    cur = pl.BlockSpec((None, None, tq, w), lambda bi, c, i: (bi, c, i, 0))
    prev = pl.BlockSpec((None, None, BAND_HALF, w), lambda bi, c, i: (bi, c, jnp.maximum(i * r - 1, 0), 0))
    nxt = pl.BlockSpec((None, None, BAND_HALF, w),
                       lambda bi, c, i: (bi, c, jnp.minimum((i + 1) * r, nb - 1), 0))
    return pl.pallas_call(
        functools.partial(_band_attn_kernel, seg_len=seg),
        grid=(b, dil, seg // tq),
        in_specs=[cur, prev, cur, nxt, prev, cur, nxt],
        out_specs=[cur, cur],
        out_shape=[jax.ShapeDtypeStruct((b, dil, seg, w), F32)] * 2,
        compiler_params=_cparams(("parallel", "parallel", "arbitrary")),
        name="band_attn",
    )(q, k, k, k, v, v, v)


def _layer_norm(y, g, b):
    mu = jnp.mean(y, axis=-1, keepdims=True)
    yc = y - mu
    var = jnp.mean(yc * yc, axis=-1, keepdims=True)
    return yc * lax.rsqrt(var + LN_EPS) * g + b


def _to_token_order(src_ref, dst_ref):
    dil, rows, w = src_ref.shape
    for c in range(dil):
        for s in range(w // LANES):
            dst_ref[s, pl.ds(c, rows, stride=dil), :] = src_ref[c, :, s * LANES:(s + 1) * LANES]


def _out_proj_kernel(x_ref, oa_ref, ob_ref, o1_ref, o2_ref, o3_ref, l1_ref, l2_ref, l3_ref,
                     wa_ref, wb_ref, wc_ref, g_ref, b_ref, y_ref, o2_scr, o3_scr, l2_scr, l3_scr):
    _to_token_order(o2_ref, o2_scr)
    _to_token_order(o3_ref, o3_scr)
    _to_token_order(l2_ref, l2_scr)
    _to_token_order(l3_ref, l3_scr)
    heads = []
    for s in range(DIL_HEADS):
        sl = slice(s * LANES, (s + 1) * LANES)
        l1, l2, l3 = l1_ref[0, :, sl], l2_scr[s], l3_scr[s]
        m = jnp.maximum(jnp.maximum(l1, l2), l3)
        e1, e2, e3 = jnp.exp(l1 - m), jnp.exp(l2 - m), jnp.exp(l3 - m)
        oc = (e1 * o1_ref[0, :, sl] + e2 * o2_scr[s] + e3 * o3_scr[s]) / (e1 + e2 + e3)
        heads.append(oc.astype(BF))
    mix = (_dot(oa_ref[...], wa_ref[...]) + _dot(ob_ref[...], wb_ref[...])
           + _dot(jnp.concatenate(heads, axis=-1), wc_ref[...]))
    y_ref[...] = _layer_norm(DEEPNORM_ALPHA * x_ref[...] + mix, g_ref[...], b_ref[...])


def _out_proj(x2d, seq, oa, ob, band, wa, wb, wc, g, bb):
    t = x2d.shape[0]
    tm = TM_PROJ
    pos_blocks = seq // tm
    w = DIL_HEADS * LANES
    row = lambda a: pl.BlockSpec((tm, a.shape[1]), lambda i: (i, 0))
    full = lambda a: pl.BlockSpec(a.shape, lambda i: (0, 0))
    cls = lambda a: pl.BlockSpec((None, a.shape[1], tm // a.shape[1], w),
                                 lambda i: (i // pos_blocks, 0, i % pos_blocks, 0))
    (o1, l1), (o2, l2), (o3, l3) = band
    slab = pltpu.VMEM((DIL_HEADS, tm, LANES), F32)
    return pl.pallas_call(
        _out_proj_kernel,
        grid=(t // tm,),
        in_specs=[row(x2d), row(oa), row(ob), cls(o1), cls(o2), cls(o3), cls(l1), cls(l2), cls(l3),
                  full(wa), full(wb), full(wc), full(g), full(bb)],
        out_specs=row(x2d),
        out_shape=jax.ShapeDtypeStruct((t, D_MODEL), F32),
        scratch_shapes=[slab, slab, slab, slab],
        compiler_params=_cparams(("parallel",)),
        name="out_proj_ln",
    )(x2d, oa, ob, o1, o2, o3, l1, l2, l3, wa, wb, wc, g, bb)


def _extract16(s, want_rank):
    row = lax.broadcasted_iota(jnp.int32, (PEER_TOPK, s.shape[1]), 0)
    work, top = s, jnp.zeros((PEER_TOPK, s.shape[1]), F32)
    rank = jnp.full(s.shape, float(PEER_TOPK), F32)
    for kk in range(PEER_TOPK):
        m = jnp.max(work, axis=0, keepdims=True)
        sel = work == m
        work = jnp.where(sel, -jnp.inf, work)
        top = jnp.where(row == kk, m, top)
        if want_rank:
            rank = jnp.where(sel, float(kk), rank)
    return top, rank


def _route_chunk(s1, s2):
    t1, _ = _extract16(s1, False)
    t2, rank2 = _extract16(s2, True)
    cand = jnp.concatenate(
        [t1[0:1] + t2]
        + [t1[k1:k1 + 1] + t2[0:8] for k1 in range(1, 8)]
        + [t1[8:16] + t2[0:1]], axis=0)
    c0 = t1[0:1] + t2[0:1]
    work, z, tau = cand, jnp.zeros_like(c0), c0
    for _ in range(PEER_TOPK):
        tau = jnp.max(work, axis=0, keepdims=True)
        work = jnp.where(work == tau, -jnp.inf, work)
        z = z + jnp.exp(tau - c0)
    cnt_of_rank = jnp.zeros_like(t1)
    for k2 in range(PEER_TOPK):
        cnt_of_rank = cnt_of_rank + jnp.where(t1 + t2[k2:k2 + 1] >= tau, 1.0, 0.0)
    cnt = jnp.zeros_like(s1)
    for k1 in range(PEER_TOPK):
        cnt = jnp.where(s1 == t1[k1:k1 + 1], cnt_of_rank[k1:k1 + 1], cnt)
    e1 = jnp.exp(s1 - t1[0:1]) / z
    e2 = jnp.exp(s2 - t2[0:1])
    return e1, cnt, e2, rank2


def _peer_route_kernel(x_ref, wq_ref, keys_ref, e1_ref, cnt_ref, k2_ref, q_scr, s_scr):
    nk = PEER_N_KEYS
    n_chunks = e1_ref.shape[0]
    q = _dot(x_ref[...].astype(BF), wq_ref[...]).astype(BF)
    for s in range(2 * PEER_HEADS):
        q_scr[s] = q[:, s * nk:(s + 1) * nk]

    def head_body(h, carry):
        s1 = _dot_nt(keys_ref[2 * h], q_scr[2 * h])
        s2 = _dot_nt(keys_ref[2 * h + 1], q_scr[2 * h + 1])
        for c in range(n_chunks):
            s_scr[0, c] = s1[:, c * LANES:(c + 1) * LANES]
            s_scr[1, c] = s2[:, c * LANES:(c + 1) * LANES]
        rows = pl.ds(pl.multiple_of(h * nk, nk), nk)

        def chunk_body(c, carry2):
            e1, cnt, e2, rank2 = _route_chunk(s_scr[0, c], s_scr[1, c])
            e1_ref[c, rows, :] = e1
            cnt_ref[c, rows, :] = cnt
            tiles = jnp.concatenate([e2.reshape(nk // SUBLANES, SUBLANES, LANES),
                                     rank2.reshape(nk // SUBLANES, SUBLANES, LANES)], axis=1)
            k2_ref[c, pl.ds(pl.multiple_of(h * 2 * nk, 2 * nk), 2 * nk), :] = tiles.reshape(2 * nk, LANES)
            return carry2

        lax.fori_loop(0, n_chunks, chunk_body, 0)
        return carry

    lax.fori_loop(0, PEER_HEADS, head_body, 0)


def _peer_route(x2d, wq, keys):
    t = x2d.shape[0]
    tt = TT_PEER
    nc = tt // LANES
    rows = PEER_HEADS * PEER_N_KEYS
    out = lambda n: pl.BlockSpec((nc, n * rows, LANES), lambda i: (i, 0, 0))
    shape = lambda n: jax.ShapeDtypeStruct((t // LANES, n * rows, LANES), F32)
    return pl.pallas_call(
        _peer_route_kernel,
        grid=(t // tt,),
        in_specs=[pl.BlockSpec((tt, D_MODEL), lambda i: (i, 0)),
                  pl.BlockSpec(wq.shape, lambda i: (0, 0)),
                  pl.BlockSpec(keys.shape, lambda i: (0, 0, 0))],
        out_specs=[out(1), out(1), out(2)],
        out_shape=[shape(1), shape(1), shape(2)],
        scratch_shapes=[pltpu.VMEM((2 * PEER_HEADS, tt, PEER_N_KEYS), BF),
                        pltpu.VMEM((2, nc, PEER_N_KEYS, LANES), F32)],
        compiler_params=_cparams(("parallel",)),
        name="peer_route",
    )(x2d, wq, keys)


def _gelu(x):
    return 0.5 * x * (1.0 + lax.erf(x * (2.0 ** -0.5)))


def _peer_expert_kernel(x_ref, e1_ref, cnt_ref, k2_ref, u_ref, vt_ref, g_ref, b_ref,
                        y_ref, acc_ref, a_ref, xb_ref):
    j = pl.program_id(1)
    nk = PEER_N_KEYS
    rows_per_step = u_ref.shape[0] // nk
    n_chunks = e1_ref.shape[0]

    @pl.when(j == 0)
    def _():
        acc_ref[...] = jnp.zeros_like(acc_ref)
        xb_ref[...] = x_ref[...].astype(BF)

    ht = _dot_nt(u_ref[...], xb_ref[...])
    n_tiles = nk // SUBLANES
    for c in range(n_chunks):
        cs = slice(c * LANES, (c + 1) * LANES)
        for r0 in range(0, rows_per_step, ROW_GROUP):
            w = [[jnp.zeros((SUBLANES, LANES), F32)] * n_tiles for _ in range(ROW_GROUP)]
            for h in range(PEER_HEADS):
                a = h * nk + j * rows_per_step + r0
                e1 = [jnp.broadcast_to(e1_ref[c, pl.ds(a + r, 1), :], (SUBLANES, LANES))
                      for r in range(ROW_GROUP)]
                cnt = [jnp.broadcast_to(cnt_ref[c, pl.ds(a + r, 1), :], (SUBLANES, LANES))
                       for r in range(ROW_GROUP)]
                for k in range(n_tiles):
                    first = (2 * h * n_tiles + 2 * k) * SUBLANES
                    pair = k2_ref[c, first:first + 2 * SUBLANES, :]
                    e2, rk2 = pair[:SUBLANES], pair[SUBLANES:]
                    for r in range(ROW_GROUP):
                        w[r][k] = w[r][k] + jnp.where(rk2 < cnt[r], e1[r] * e2, 0.0)
            for r in range(ROW_GROUP):
                for k in range(0, n_tiles, 2):
                    bs = slice((r0 + r) * nk + k * SUBLANES, (r0 + r) * nk + (k + 2) * SUBLANES)
                    gate = jnp.concatenate([w[r][k], w[r][k + 1]], axis=0)
                    a_ref[bs, cs] = (gate * _gelu(ht[bs, cs])).astype(BF)
    acc_ref[...] += _dot(vt_ref[...], a_ref[...])

    @pl.when(j == pl.num_programs(1) - 1)
    def _():
        ffn = acc_ref[...].T
        y_ref[...] = _layer_norm(DEEPNORM_ALPHA * x_ref[...] + ffn, g_ref[...], b_ref[...])


def _peer_expert(x2d, e1, cnt, k2, u, vt, g, bb):
    t = x2d.shape[0]
    tt, eb = TT_PEER, EB_PEER
    nc = tt // LANES
    rows = PEER_HEADS * PEER_N_KEYS
    route = lambda n: pl.BlockSpec((nc, n * rows, LANES), lambda i, j: (i, 0, 0))
    vec = pl.BlockSpec((1, D_MODEL), lambda i, j: (0, 0))
    return pl.pallas_call(
        _peer_expert_kernel,
        grid=(t // tt, PEER_N_EXPERTS // eb),
        in_specs=[pl.BlockSpec((tt, D_MODEL), lambda i, j: (i, 0)),
                  route(1), route(1), route(2),
                  pl.BlockSpec((eb, D_MODEL), lambda i, j: (j, 0)),
                  pl.BlockSpec((D_MODEL, eb), lambda i, j: (0, j)),
                  vec, vec],
        out_specs=pl.BlockSpec((tt, D_MODEL), lambda i, j: (i, 0)),
        out_shape=jax.ShapeDtypeStruct((t, D_MODEL), F32),
        scratch_shapes=[pltpu.VMEM((D_MODEL, tt), F32), pltpu.VMEM((eb, tt), BF),
                        pltpu.VMEM((tt, D_MODEL), BF)],
        compiler_params=_cparams(("parallel", "arbitrary")),
        name="peer_expert_ln",
    )(x2d, e1, cnt, k2, u, vt, g, bb)


def _split_w_in(w):
    sizes = (256, 256, 256, MLA_Q_RANK, MLA_KV_RANK, MLA_ROPE_DIM, 384, 384, 384)
    out, off = [], 0
    for n in sizes:
        out.append(w[:, off:off + n])
        off += n
    return out


def _pad_rows(w, n_heads, d):
    n = w.shape[1]
    w3 = w.reshape(n_heads, d, n)
    return jnp.zeros((n_heads, LANES, n), w.dtype).at[:, :d, :].set(w3).reshape(n_heads * LANES, n)


def _layer(x2d, batch, seq, l, w_in, diff_lambda, diff_subln_g, mla_q_norm_g, mla_kv_norm_g, mla_w_uq,
           mla_w_ukv, w_o, ln1_g, ln1_b, peer_w_q, peer_sub_keys, peer_u, peer_v, ln2_g, ln2_b):
    lambda_init = 0.8 - 0.6 * math.exp(-0.3 * (l + 1))
    a_q, a_k, a_v, b_cq, b_ckv, b_kr, c_q, c_k, c_v = _split_w_in(w_in)
    bf = lambda a: a.astype(BF)

    cq, sq = _rope_tables(seq, DIFF_QK_DIM, 0, LOG2E * DIFF_QK_DIM ** -0.5, 0.0)
    ck, sk = _rope_tables(seq, DIFF_QK_DIM, 0, 1.0, 0.0)
    nh = 2 * DIFF_HEADS
    qa, ka, va = _proj_qkv(
        x2d, seq,
        bf(_pad_heads(a_q, nh, DIFF_QK_DIM)), bf(_pad_heads(_rot_cols(a_q, nh, DIFF_QK_DIM), nh, DIFF_QK_DIM)),
        bf(_pad_heads(a_k, nh, DIFF_QK_DIM)), bf(_pad_heads(_rot_cols(a_k, nh, DIFF_QK_DIM), nh, DIFF_QK_DIM)),
        bf(_pad_heads(a_v, DIFF_HEADS, DIFF_V_DIM)), cq, sq, ck, sk)
    g_pad = jnp.zeros((1, LANES), F32).at[0, :DIFF_V_DIM].set(diff_subln_g.astype(F32))
    sh = lambda a: a.reshape(batch, seq, a.shape[-1])
    out_a = _diff_attn(sh(qa), sh(ka), sh(va), diff_lambda.astype(F32), g_pad, lambda_init)

    qk_dim = MLA_NOPE_DIM + MLA_ROPE_DIM
    cqm, sqm = _rope_tables(seq, MLA_ROPE_DIM, MLA_NOPE_DIM, LOG2E * qk_dim ** -0.5, 1.0)
    cqm = cqm.at[:, qk_dim:].set(0.0)
    ckm, skm = _rope_tables(seq, MLA_ROPE_DIM, MLA_NOPE_DIM, 1.0, 0.0)
    uq3 = mla_w_uq.reshape(MLA_Q_RANK, MLA_HEADS, qk_dim)
    uq_rope = uq3[..., MLA_NOPE_DIM:].reshape(MLA_Q_RANK, MLA_HEADS * MLA_ROPE_DIM)
    uq_rot = _pad_heads(_rot_cols(uq_rope, MLA_HEADS, MLA_ROPE_DIM), MLA_HEADS, MLA_ROPE_DIM, MLA_NOPE_DIM)
    ukv3 = mla_w_ukv.reshape(MLA_KV_RANK, MLA_HEADS, MLA_NOPE_DIM + MLA_V_DIM)
    uk = ukv3[..., :MLA_NOPE_DIM].reshape(MLA_KV_RANK, MLA_HEADS * MLA_NOPE_DIM)
    uv = ukv3[..., MLA_NOPE_DIM:].reshape(MLA_KV_RANK, MLA_HEADS * MLA_V_DIM)
    qb, kb, vb = _proj_mla(
        x2d, seq, bf(b_cq), bf(b_ckv),
        bf(_pad_heads(b_kr, 1, MLA_ROPE_DIM, MLA_NOPE_DIM)),
        bf(_pad_heads(_rot_cols(b_kr, 1, MLA_ROPE_DIM), 1, MLA_ROPE_DIM, MLA_NOPE_DIM)),
        mla_q_norm_g.astype(F32)[None, :], mla_kv_norm_g.astype(F32)[None, :],
        bf(_pad_heads(mla_w_uq, MLA_HEADS, qk_dim)), bf(uq_rot),
        bf(_pad_heads(uk, MLA_HEADS, MLA_NOPE_DIM)), bf(_pad_heads(uv, MLA_HEADS, MLA_V_DIM)),
        cqm, sqm, ckm, skm)
    out_b = _mla_attn(sh(qb), sh(kb), sh(vb))

    cqd, sqd = _rope_tables(seq, DIL_HEAD_DIM, 0, LOG2E * DIL_HEAD_DIM ** -0.5, 0.0)
    ckd, skd = _rope_tables(seq, DIL_HEAD_DIM, 0, 1.0, 0.0)
    nh = DIL_HEADS
    dil_qkv = _proj_dil(
        x2d, batch, seq,
        bf(_pad_heads(c_q, nh, DIL_HEAD_DIM)), bf(_pad_heads(_rot_cols(c_q, nh, DIL_HEAD_DIM), nh, DIL_HEAD_DIM)),
        bf(_pad_heads(c_k, nh, DIL_HEAD_DIM)), bf(_pad_heads(_rot_cols(c_k, nh, DIL_HEAD_DIM), nh, DIL_HEAD_DIM)),
        bf(_pad_heads(c_v, nh, DIL_HEAD_DIM)), cqd, sqd, ckd, skd)
    band = [_band_attn(q, k, v) for q, k, v in dil_qkv]
    fl = lambda a: a.reshape(batch * seq, a.shape[-1])

    wo_a = _pad_rows(w_o[:DIFF_HEADS * DIFF_V_DIM], DIFF_HEADS, DIFF_V_DIM)
    wo_b = _pad_rows(w_o[256:256 + MLA_HEADS * MLA_V_DIM], MLA_HEADS, MLA_V_DIM)
    wo_c = _pad_rows(w_o[640:], DIL_HEADS, DIL_HEAD_DIM)
    x1 = _out_proj(x2d, seq, fl(out_a), fl(out_b), band,
                   bf(wo_a), bf(wo_b), bf(wo_c), ln1_g.astype(F32)[None, :], ln1_b.astype(F32)[None, :])

    keys = bf(peer_sub_keys.reshape(PEER_HEADS * 2, PEER_N_KEYS, PEER_HALF_DIM))
    e1, cnt, k2 = _peer_route(x1, bf(peer_w_q), keys)
    return _peer_expert(x1, e1, cnt, k2, bf(peer_u), bf(peer_v).T,
                        ln2_g.astype(F32)[None, :], ln2_b.astype(F32)[None, :])


def kernel(x, w_in, diff_lambda, diff_subln_g, mla_q_norm_g, mla_kv_norm_g, mla_w_uq, mla_w_ukv, w_o,
           ln1_g, ln1_b, peer_w_q, peer_sub_keys, peer_u, peer_v, ln2_g, ln2_b):
    batch, seq, d = x.shape
    params = (w_in, diff_lambda, diff_subln_g, mla_q_norm_g, mla_kv_norm_g, mla_w_uq, mla_w_ukv, w_o,
              ln1_g, ln1_b, peer_w_q, peer_sub_keys, peer_u, peer_v, ln2_g, ln2_b)
    x2d = x.reshape(batch * seq, d)
    for l in range(DEPTH):
        x2d = _layer(x2d, batch, seq, l, *[p[l] for p in params])
    return x2d.reshape(batch, seq, d)
```

```python
import functools
import math

import jax
import jax.numpy as jnp
from jax import lax
from jax.experimental import pallas as pl
from jax.experimental.pallas import tpu as pltpu

BF = jnp.bfloat16
F32 = jnp.float32

LANES = 128
SUBLANES = 8
D_MODEL = 1024
DEPTH = 2
DIFF_HEADS = 4
DIFF_QK_DIM = 32
DIFF_V_DIM = 64
MLA_HEADS = 6
MLA_Q_RANK = 256
MLA_KV_RANK = 128
MLA_NOPE_DIM = 64
MLA_ROPE_DIM = 32
MLA_V_DIM = 64
DIL_HEADS = 6
DIL_HEAD_DIM = 64
DIL_PAIRS = ((128, 1), (512, 4), (2048, 16))
ROPE_THETA = 10000.0
PEER_HEADS = 8
PEER_N_KEYS = 128
PEER_N_EXPERTS = PEER_N_KEYS * PEER_N_KEYS
PEER_HALF_DIM = 128
PEER_TOPK = 16
DEEPNORM_ALPHA = (2 * DEPTH) ** 0.25
LN_EPS = 1e-5
NEG_INF = -1e30
LOG2E = math.log2(math.e)
LN2 = math.log(2.0)
SQRT_HALF = 0.5 ** 0.5

VMEM_LIMIT = 56 * 1024 * 1024

TM_PROJ = 512
TQ_DENSE = 256
TQ_BAND = 256
BAND_HALF = 64
TT_PEER = 512
EB_PEER = 1024
ROW_GROUP = 2


def _cparams(sem):
    return pltpu.CompilerParams(dimension_semantics=sem, vmem_limit_bytes=VMEM_LIMIT)


def _dot(a, b):
    return jnp.dot(a, b, preferred_element_type=F32)


def _dot_nt(a, b):
    return lax.dot_general(a, b, (((1,), (1,)), ((), ())), preferred_element_type=F32)


def _pad_heads(w, n_heads, d, offset=0):
    k = w.shape[0]
    w3 = w.reshape(k, n_heads, d)
    out = jnp.zeros((k, n_heads, LANES), w.dtype).at[:, :, offset:offset + d].set(w3)
    return out.reshape(k, n_heads * LANES)


def _rot_cols(w, n_heads, d):
    k = w.shape[0]
    w3 = w.reshape(k, n_heads, d)
    half = d // 2
    return jnp.concatenate([-w3[..., half:], w3[..., :half]], axis=-1).reshape(k, n_heads * d)


def _rope_tables(seq, d, offset, scale, fill):
    half = d // 2
    inv_freq = jnp.exp(-math.log(ROPE_THETA) * jnp.arange(half, dtype=F32) / half)
    ang = jnp.arange(seq, dtype=jnp.int32).astype(F32)[:, None] * inv_freq[None, :]
    cos, sin = jnp.cos(ang), jnp.sin(ang)
    cos_t = jnp.full((seq, LANES), fill, F32).at[:, offset:offset + d].set(jnp.concatenate([cos, cos], -1))
    sin_t = jnp.zeros((seq, LANES), F32).at[:, offset:offset + d].set(jnp.concatenate([sin, sin], -1))
    return cos_t * scale, sin_t * scale


def _rope_store(o_ref, h, hr, cos, sin):
    for s in range(o_ref.shape[1] // LANES):
        sl = slice(s * LANES, (s + 1) * LANES)
        o_ref[:, sl] = (h[:, sl] * cos + hr[:, sl] * sin).astype(o_ref.dtype)


def _proj_qkv_kernel(x_ref, wq_ref, wqr_ref, wk_ref, wkr_ref, wv_ref, cq_ref, sq_ref, ck_ref, sk_ref,
                     q_ref, k_ref, v_ref):
    xb = x_ref[...].astype(BF)
    _rope_store(q_ref, _dot(xb, wq_ref[...]), _dot(xb, wqr_ref[...]), cq_ref[...], sq_ref[...])
    _rope_store(k_ref, _dot(xb, wk_ref[...]), _dot(xb, wkr_ref[...]), ck_ref[...], sk_ref[...])
    v_ref[...] = _dot(xb, wv_ref[...]).astype(v_ref.dtype)


def _proj_qkv(x2d, seq, wq, wqr, wk, wkr, wv, cq, sq, ck, sk):
    t = x2d.shape[0]
    tm = TM_PROJ
    nq, nv = wq.shape[1], wv.shape[1]
    pos_blocks = seq // tm
    full = lambda a: pl.BlockSpec(a.shape, lambda i: (0, 0))
    tab = pl.BlockSpec((tm, LANES), lambda i: (i % pos_blocks, 0))
    return pl.pallas_call(
        _proj_qkv_kernel,
        grid=(t // tm,),
        in_specs=[pl.BlockSpec((tm, D_MODEL), lambda i: (i, 0)),
                  full(wq), full(wqr), full(wk), full(wkr), full(wv), tab, tab, tab, tab],
        out_specs=[pl.BlockSpec((tm, nq), lambda i: (i, 0)),
                   pl.BlockSpec((tm, nq), lambda i: (i, 0)),
                   pl.BlockSpec((tm, nv), lambda i: (i, 0))],
        out_shape=[jax.ShapeDtypeStruct((t, nq), BF), jax.ShapeDtypeStruct((t, nq), BF),
                   jax.ShapeDtypeStruct((t, nv), BF)],
        compiler_params=_cparams(("parallel",)),
        name="proj_qkv",
    )(x2d, wq, wqr, wk, wkr, wv, cq, sq, ck, sk)


def _proj_dil_kernel(x_ref, wq_ref, wqr_ref, wk_ref, wkr_ref, wv_ref, cq_ref, sq_ref, ck_ref, sk_ref,
                     *refs):
    outs, scr = refs[:-1], refs[-1]
    n_slabs = scr.shape[0]
    xb = x_ref[...].astype(BF)

    def rope(h, hr, cos, sin):
        return jnp.concatenate(
            [h[:, s * LANES:(s + 1) * LANES] * cos + hr[:, s * LANES:(s + 1) * LANES] * sin
             for s in range(n_slabs)], axis=-1)

    def emit(h, which):
        for s in range(n_slabs):
            scr[s] = h[:, s * LANES:(s + 1) * LANES]
        for p in range(len(outs) // 3):
            o_ref = outs[3 * p + which]
            dil, rows, _ = o_ref.shape
            for c in range(dil):
                for s in range(n_slabs):
                    o_ref[c, :, s * LANES:(s + 1) * LANES] = (
                        scr[s, pl.ds(c, rows, stride=dil), :].astype(o_ref.dtype))

    emit(rope(_dot(xb, wq_ref[...]), _dot(xb, wqr_ref[...]), cq_ref[...], sq_ref[...]), 0)
    emit(rope(_dot(xb, wk_ref[...]), _dot(xb, wkr_ref[...]), ck_ref[...], sk_ref[...]), 1)
    emit(_dot(xb, wv_ref[...]), 2)


def _proj_dil(x2d, batch, seq, wq, wqr, wk, wkr, wv, cq, sq, ck, sk):
    t = x2d.shape[0]
    tm = TM_PROJ
    w = wq.shape[1]
    pos_blocks = seq // tm
    full = lambda a: pl.BlockSpec(a.shape, lambda i: (0, 0))
    tab = pl.BlockSpec((tm, LANES), lambda i: (i % pos_blocks, 0))
    out_specs, out_shape = [], []
    for _, dil in DIL_PAIRS:
        for _ in range(3):
            out_specs.append(pl.BlockSpec((None, dil, tm // dil, w),
                                          lambda i: (i // pos_blocks, 0, i % pos_blocks, 0)))
            out_shape.append(jax.ShapeDtypeStruct((batch, dil, seq // dil, w), BF))
    outs = pl.pallas_call(
        _proj_dil_kernel,
        grid=(t // tm,),
        in_specs=[pl.BlockSpec((tm, D_MODEL), lambda i: (i, 0)),
                  full(wq), full(wqr), full(wk), full(wkr), full(wv), tab, tab, tab, tab],
        out_specs=out_specs,
        out_shape=out_shape,
        scratch_shapes=[pltpu.VMEM((w // LANES, tm, LANES), F32)],
        compiler_params=_cparams(("parallel",)),
        name="proj_dil",
    )(x2d, wq, wqr, wk, wkr, wv, cq, sq, ck, sk)
    return [outs[3 * p:3 * p + 3] for p in range(len(DIL_PAIRS))]


def _rms(x, g, n):
    return x * lax.rsqrt(jnp.sum(x * x, axis=-1, keepdims=True) / n + LN_EPS) * g


def _proj_mla_kernel(x_ref, wcq_ref, wckv_ref, wkr_ref, wkrr_ref, gq_ref, gkv_ref,
                     wuq_ref, wuqr_ref, wuk_ref, wuv_ref, cq_ref, sq_ref, ck_ref, sk_ref,
                     q_ref, k_ref, v_ref):
    xb = x_ref[...].astype(BF)
    nq = _rms(_dot(xb, wcq_ref[...]), gq_ref[...], MLA_Q_RANK).astype(BF)
    nkv = _rms(_dot(xb, wckv_ref[...]), gkv_ref[...], MLA_KV_RANK).astype(BF)
    _rope_store(q_ref, _dot(nq, wuq_ref[...]), _dot(nq, wuqr_ref[...]), cq_ref[...], sq_ref[...])
    kpe = _dot(xb, wkr_ref[...]) * ck_ref[...] + _dot(xb, wkrr_ref[...]) * sk_ref[...]
    kn = _dot(nkv, wuk_ref[...])
    for s in range(MLA_HEADS):
        sl = slice(s * LANES, (s + 1) * LANES)
        k_ref[:, sl] = (kn[:, sl] + kpe).astype(k_ref.dtype)
    v_ref[...] = _dot(nkv, wuv_ref[...]).astype(v_ref.dtype)


def _proj_mla(x2d, seq, wcq, wckv, wkr, wkrr, gq, gkv, wuq, wuqr, wuk, wuv, cq, sq, ck, sk):
    t = x2d.shape[0]
    tm = TM_PROJ
    n = MLA_HEADS * LANES
    pos_blocks = seq // tm
    full = lambda a: pl.BlockSpec(a.shape, lambda i: (0, 0))
    tab = pl.BlockSpec((tm, LANES), lambda i: (i % pos_blocks, 0))
    out = pl.BlockSpec((tm, n), lambda i: (i, 0))
    return pl.pallas_call(
        _proj_mla_kernel,
        grid=(t // tm,),
        in_specs=[pl.BlockSpec((tm, D_MODEL), lambda i: (i, 0)),
                  full(wcq), full(wckv), full(wkr), full(wkrr), full(gq), full(gkv),
                  full(wuq), full(wuqr), full(wuk), full(wuv), tab, tab, tab, tab],
        out_specs=[out, out, out],
        out_shape=[jax.ShapeDtypeStruct((t, n), BF)] * 3,
        compiler_params=_cparams(("parallel",)),
        name="proj_mla",
    )(x2d, wcq, wckv, wkr, wkrr, gq, gkv, wuq, wuqr, wuk, wuv, cq, sq, ck, sk)


def _softmax_pv(q, k, v):
    s = _dot_nt(q, k)
    m = jnp.max(s, axis=-1, keepdims=True)
    p = jnp.exp2(s - m)
    l = jnp.sum(p, axis=-1, keepdims=True)
    return _dot(p.astype(BF), v) / l


def _diff_attn_kernel(lam_ref, g_ref, q1_ref, q2_ref, k1_ref, k2_ref, v_ref, o_ref, *, lambda_init):
    lam = lam_ref[...]
    lam_full = (jnp.exp(jnp.sum(lam[0:1] * lam[1:2], axis=-1, keepdims=True))
                - jnp.exp(jnp.sum(lam[2:3] * lam[3:4], axis=-1, keepdims=True)) + lambda_init)
    v = v_ref[0]
    o1 = _softmax_pv(q1_ref[0], k1_ref[0], v)
    o2 = _softmax_pv(q2_ref[0], k2_ref[0], v)
    o = o1 - lam_full * o2
    o_ref[0] = (_rms(o, g_ref[...], DIFF_V_DIM) * (1.0 - lambda_init)).astype(o_ref.dtype)


def _diff_attn(q, k, v, lam, g_pad, lambda_init):
    b, s, _ = q.shape
    tq = TQ_DENSE
    qspec = lambda off: pl.BlockSpec((1, tq, LANES), lambda bi, h, i: (bi, i, 2 * h + off))
    kspec = lambda off: pl.BlockSpec((1, s, LANES), lambda bi, h, i: (bi, 0, 2 * h + off))
    return pl.pallas_call(
        functools.partial(_diff_attn_kernel, lambda_init=lambda_init),
        grid=(b, DIFF_HEADS, s // tq),
        in_specs=[pl.BlockSpec(lam.shape, lambda bi, h, i: (0, 0)),
                  pl.BlockSpec(g_pad.shape, lambda bi, h, i: (0, 0)),
                  qspec(0), qspec(1), kspec(0), kspec(1),
                  pl.BlockSpec((1, s, LANES), lambda bi, h, i: (bi, 0, h))],
        out_specs=pl.BlockSpec((1, tq, LANES), lambda bi, h, i: (bi, i, h)),
        out_shape=jax.ShapeDtypeStruct((b, s, DIFF_HEADS * LANES), BF),
        compiler_params=_cparams(("parallel", "parallel", "arbitrary")),
        name="diff_attn",
    )(lam, g_pad, q, q, k, k, v)


def _mla_attn_kernel(q_ref, k_ref, v_ref, o_ref):
    o_ref[0] = _softmax_pv(q_ref[0], k_ref[0], v_ref[0]).astype(o_ref.dtype)


def _mla_attn(q, k, v):
    b, s, _ = q.shape
    tq = TQ_DENSE
    kv = pl.BlockSpec((1, s, LANES), lambda bi, h, i: (bi, 0, h))
    qo = pl.BlockSpec((1, tq, LANES), lambda bi, h, i: (bi, i, h))
    return pl.pallas_call(
        _mla_attn_kernel,
        grid=(b, MLA_HEADS, s // tq),
        in_specs=[qo, kv, kv],
        out_specs=qo,
        out_shape=jax.ShapeDtypeStruct((b, s, MLA_HEADS * LANES), BF),
        compiler_params=_cparams(("parallel", "parallel", "arbitrary")),
        name="mla_attn",
    )(q, k, v)


def _band_attn_kernel(q_ref, kp_ref, kc_ref, kn_ref, vp_ref, vc_ref, vn_ref, o_ref, lse_ref, *, seg_len):
    i = pl.program_id(2)
    tq = q_ref.shape[0]
    blk = LANES
    span = blk + 2 * BAND_HALF
    kk = jnp.concatenate([kp_ref[...], kc_ref[...], kn_ref[...]], axis=0)
    vv = jnp.concatenate([vp_ref[...], vc_ref[...], vn_ref[...]], axis=0)
    qi = lax.broadcasted_iota(jnp.int32, (blk, span), 0)
    kj = lax.broadcasted_iota(jnp.int32, (blk, span), 1)
    for j in range(tq // blk):
        qpos = i * tq + j * blk + qi
        kpos = i * tq + j * blk - BAND_HALF + kj
        mask = (jnp.abs(kpos - qpos) <= BAND_HALF) & (kpos >= 0) & (kpos < seg_len)
        for h in range(DIL_HEADS):
            cs = slice(h * LANES, (h + 1) * LANES)
            rows = slice(j * blk, (j + 1) * blk)
            k = kk[j * blk:j * blk + span, cs]
            v = vv[j * blk:j * blk + span, cs]
            s = jnp.where(mask, _dot_nt(q_ref[rows, cs], k), NEG_INF)
            m = jnp.max(s, axis=-1, keepdims=True)
            p = jnp.exp2(s - m)
            l = jnp.sum(p, axis=-1, keepdims=True)
            o_ref[rows, cs] = _dot(p.astype(BF), v) / l
            lse_ref[rows, cs] = jnp.broadcast_to(m * LN2 + jnp.log(l), (blk, LANES))


def _band_attn(q, k, v):
    b, dil, seg, w = q.shape
    tq = TQ_BAND
    r = tq // BAND_HALF
    nb = seg // BAND_HALF
    cur = pl.BlockSpec((None, None, tq, w), lambda bi, c, i: (bi, c, i, 0))
    prev = pl.BlockSpec((None, None, BAND_HALF, w), lambda bi, c, i: (bi, c, jnp.maximum(i * r - 1, 0), 0))
    nxt = pl.BlockSpec((None, None, BAND_HALF, w),
                       lambda bi, c, i: (bi, c, jnp.minimum((i + 1) * r, nb - 1), 0))
    return pl.pallas_call(
        functools.partial(_band_attn_kernel, seg_len=seg),
        grid=(b, dil, seg // tq),
        in_specs=[cur, prev, cur, nxt, prev, cur, nxt],
        out_specs=[cur, cur],
        out_shape=[jax.ShapeDtypeStruct((b, dil, seg, w), F32)] * 2,
        compiler_params=_cparams(("parallel", "parallel", "arbitrary")),
        name="band_attn",
    )(q, k, k, k, v, v, v)


def _layer_norm(y, g, b):
    mu = jnp.mean(y, axis=-1, keepdims=True)
    yc = y - mu
    var = jnp.mean(yc * yc, axis=-1, keepdims=True)
    return yc * lax.rsqrt(var + LN_EPS) * g + b


def _to_token_order(src_ref, dst_ref):
    dil, rows, w = src_ref.shape
    for c in range(dil):
        for s in range(w // LANES):
            dst_ref[s, pl.ds(c, rows, stride=dil), :] = src_ref[c, :, s * LANES:(s + 1) * LANES]


def _out_proj_kernel(x_ref, oa_ref, ob_ref, o1_ref, o2_ref, o3_ref, l1_ref, l2_ref, l3_ref,
                     wa_ref, wb_ref, wc_ref, g_ref, b_ref, y_ref, o2_scr, o3_scr, l2_scr, l3_scr):
    _to_token_order(o2_ref, o2_scr)
    _to_token_order(o3_ref, o3_scr)
    _to_token_order(l2_ref, l2_scr)
    _to_token_order(l3_ref, l3_scr)
    heads = []
    for s in range(DIL_HEADS):
        sl = slice(s * LANES, (s + 1) * LANES)
        l1, l2, l3 = l1_ref[0, :, sl], l2_scr[s], l3_scr[s]
        m = jnp.maximum(jnp.maximum(l1, l2), l3)
        e1, e2, e3 = jnp.exp(l1 - m), jnp.exp(l2 - m), jnp.exp(l3 - m)
        oc = (e1 * o1_ref[0, :, sl] + e2 * o2_scr[s] + e3 * o3_scr[s]) / (e1 + e2 + e3)
        heads.append(oc.astype(BF))
    mix = (_dot(oa_ref[...], wa_ref[...]) + _dot(ob_ref[...], wb_ref[...])
           + _dot(jnp.concatenate(heads, axis=-1), wc_ref[...]))
    y_ref[...] = _layer_norm(DEEPNORM_ALPHA * x_ref[...] + mix, g_ref[...], b_ref[...])


def _out_proj(x2d, seq, oa, ob, band, wa, wb, wc, g, bb):
    t = x2d.shape[0]
    tm = TM_PROJ
    pos_blocks = seq // tm
    w = DIL_HEADS * LANES
    row = lambda a: pl.BlockSpec((tm, a.shape[1]), lambda i: (i, 0))
    full = lambda a: pl.BlockSpec(a.shape, lambda i: (0, 0))
    cls = lambda a: pl.BlockSpec((None, a.shape[1], tm // a.shape[1], w),
                                 lambda i: (i // pos_blocks, 0, i % pos_blocks, 0))
    (o1, l1), (o2, l2), (o3, l3) = band
    slab = pltpu.VMEM((DIL_HEADS, tm, LANES), F32)
    return pl.pallas_call(
        _out_proj_kernel,
        grid=(t // tm,),
        in_specs=[row(x2d), row(oa), row(ob), cls(o1), cls(o2), cls(o3), cls(l1), cls(l2), cls(l3),
                  full(wa), full(wb), full(wc), full(g), full(bb)],
        out_specs=row(x2d),
        out_shape=jax.ShapeDtypeStruct((t, D_MODEL), F32),
        scratch_shapes=[slab, slab, slab, slab],
        compiler_params=_cparams(("parallel",)),
        name="out_proj_ln",
    )(x2d, oa, ob, o1, o2, o3, l1, l2, l3, wa, wb, wc, g, bb)


def _extract16(s, want_rank):
    row = lax.broadcasted_iota(jnp.int32, (PEER_TOPK, s.shape[1]), 0)
    work, top = s, jnp.zeros((PEER_TOPK, s.shape[1]), F32)
    rank = jnp.full(s.shape, float(PEER_TOPK), F32)
    for kk in range(PEER_TOPK):
        m = jnp.max(work, axis=0, keepdims=True)
        sel = work == m
        work = jnp.where(sel, -jnp.inf, work)
        top = jnp.where(row == kk, m, top)
        if want_rank:
            rank = jnp.where(sel, float(kk), rank)
    return top, rank


def _route_chunk(s1, s2):
    t1, _ = _extract16(s1, False)
    t2, rank2 = _extract16(s2, True)
    cand = jnp.concatenate(
        [t1[0:1] + t2]
        + [t1[k1:k1 + 1] + t2[0:8] for k1 in range(1, 8)]
        + [t1[8:16] + t2[0:1]], axis=0)
    c0 = t1[0:1] + t2[0:1]
    work, z, tau = cand, jnp.zeros_like(c0), c0
    for _ in range(PEER_TOPK):
        tau = jnp.max(work, axis=0, keepdims=True)
        work = jnp.where(work == tau, -jnp.inf, work)
        z = z + jnp.exp(tau - c0)
    cnt_of_rank = jnp.zeros_like(t1)
    for k2 in range(PEER_TOPK):
        cnt_of_rank = cnt_of_rank + jnp.where(t1 + t2[k2:k2 + 1] >= tau, 1.0, 0.0)
    cnt = jnp.zeros_like(s1)
    for k1 in range(PEER_TOPK):
        cnt = jnp.where(s1 == t1[k1:k1 + 1], cnt_of_rank[k1:k1 + 1], cnt)
    e1 = jnp.exp(s1 - t1[0:1]) / z
    e2 = jnp.exp(s2 - t2[0:1])
    return e1, cnt, e2, rank2


def _peer_route_kernel(x_ref, wq_ref, keys_ref, e1_ref, cnt_ref, k2_ref, q_scr, s_scr):
    nk = PEER_N_KEYS
    n_chunks = e1_ref.shape[0]
    q = _dot(x_ref[...].astype(BF), wq_ref[...]).astype(BF)
    for s in range(2 * PEER_HEADS):
        q_scr[s] = q[:, s * nk:(s + 1) * nk]

    def head_body(h, carry):
        s1 = _dot_nt(keys_ref[2 * h], q_scr[2 * h])
        s2 = _dot_nt(keys_ref[2 * h + 1], q_scr[2 * h + 1])
        for c in range(n_chunks):
            s_scr[0, c] = s1[:, c * LANES:(c + 1) * LANES]
            s_scr[1, c] = s2[:, c * LANES:(c + 1) * LANES]
        rows = pl.ds(pl.multiple_of(h * nk, nk), nk)

        def chunk_body(c, carry2):
            e1, cnt, e2, rank2 = _route_chunk(s_scr[0, c], s_scr[1, c])
            e1_ref[c, rows, :] = e1
            cnt_ref[c, rows, :] = cnt
            tiles = jnp.concatenate([e2.reshape(nk // SUBLANES, SUBLANES, LANES),
                                     rank2.reshape(nk // SUBLANES, SUBLANES, LANES)], axis=1)
            k2_ref[c, pl.ds(pl.multiple_of(h * 2 * nk, 2 * nk), 2 * nk), :] = tiles.reshape(2 * nk, LANES)
            return carry2

        lax.fori_loop(0, n_chunks, chunk_body, 0)
        return carry

    lax.fori_loop(0, PEER_HEADS, head_body, 0)


def _peer_route(x2d, wq, keys):
    t = x2d.shape[0]
    tt = TT_PEER
    nc = tt // LANES
    rows = PEER_HEADS * PEER_N_KEYS
    out = lambda n: pl.BlockSpec((nc, n * rows, LANES), lambda i: (i, 0, 0))
    shape = lambda n: jax.ShapeDtypeStruct((t // LANES, n * rows, LANES), F32)
    return pl.pallas_call(
        _peer_route_kernel,
        grid=(t // tt,),
        in_specs=[pl.BlockSpec((tt, D_MODEL), lambda i: (i, 0)),
                  pl.BlockSpec(wq.shape, lambda i: (0, 0)),
                  pl.BlockSpec(keys.shape, lambda i: (0, 0, 0))],
        out_specs=[out(1), out(1), out(2)],
        out_shape=[shape(1), shape(1), shape(2)],
        scratch_shapes=[pltpu.VMEM((2 * PEER_HEADS, tt, PEER_N_KEYS), BF),
                        pltpu.VMEM((2, nc, PEER_N_KEYS, LANES), F32)],
        compiler_params=_cparams(("parallel",)),
        name="peer_route",
    )(x2d, wq, keys)


def _peer_expert_kernel(x_ref, e1_ref, cnt_ref, k2_ref, u_ref, vt_ref, g_ref, b_ref,
                        y_ref, acc_ref, a_ref, xb_ref, row_ref):
    j = pl.program_id(1)
    nk = PEER_N_KEYS
    rows_per_step = u_ref.shape[0] // nk
    n_chunks = e1_ref.shape[0]

    @pl.when(j == 0)
    def _():
        acc_ref[...] = jnp.zeros_like(acc_ref)
        xb_ref[...] = (x_ref[...] * SQRT_HALF).astype(BF)

    n_tiles = nk // SUBLANES
    for c in range(n_chunks):
        for h in range(PEER_HEADS):
            rows = pl.ds(pl.multiple_of(h * nk + j * rows_per_step, rows_per_step), rows_per_step)
            e1_rows = e1_ref[c, rows, :] * SQRT_HALF
            cnt_rows = cnt_ref[c, rows, :]
            for r in range(rows_per_step):
                idx = (c * PEER_HEADS + h) * rows_per_step + r
                row_ref[0, idx] = jnp.broadcast_to(e1_rows[r:r + 1], (SUBLANES, LANES))
                row_ref[1, idx] = jnp.broadcast_to(cnt_rows[r:r + 1], (SUBLANES, LANES))

    ht = _dot_nt(u_ref[...], xb_ref[...])
    for c in range(n_chunks):
        cs = slice(c * LANES, (c + 1) * LANES)
        for r0 in range(0, rows_per_step, ROW_GROUP):
            w = [[jnp.zeros((SUBLANES, LANES), F32)] * n_tiles for _ in range(ROW_GROUP)]
            for h in range(PEER_HEADS):
                idx = (c * PEER_HEADS + h) * rows_per_step + r0
                e1 = [row_ref[0, idx + r] for r in range(ROW_GROUP)]
                cnt = [row_ref[1, idx + r] for r in range(ROW_GROUP)]
                for k in range(n_tiles):
                    first = (2 * h * n_tiles + 2 * k) * SUBLANES
                    pair = k2_ref[c, first:first + 2 * SUBLANES, :]
                    e2, rk2 = pair[:SUBLANES], pair[SUBLANES:]
                    for r in range(ROW_GROUP):
                        w[r][k] = w[r][k] + jnp.where(rk2 < cnt[r], e1[r] * e2, 0.0)
            for r in range(ROW_GROUP):
                for k in range(0, n_tiles, 2):
                    bs = slice((r0 + r) * nk + k * SUBLANES, (r0 + r) * nk + (k + 2) * SUBLANES)
                    gate = jnp.concatenate([w[r][k], w[r][k + 1]], axis=0)
                    hb = ht[bs, cs]
                    a_ref[bs, cs] = (gate * (hb * (1.0 + lax.erf(hb)))).astype(BF)
    acc_ref[...] += _dot(vt_ref[...], a_ref[...])

    @pl.when(j == pl.num_programs(1) - 1)
    def _():
        ffn = acc_ref[...].T
        y_ref[...] = _layer_norm(DEEPNORM_ALPHA * x_ref[...] + ffn, g_ref[...], b_ref[...])


def _peer_expert(x2d, e1, cnt, k2, u, vt, g, bb):
    t = x2d.shape[0]
    tt, eb = TT_PEER, EB_PEER
    nc = tt // LANES
    rows = PEER_HEADS * PEER_N_KEYS
    route = lambda n: pl.BlockSpec((nc, n * rows, LANES), lambda i, j: (i, 0, 0))
    vec = pl.BlockSpec((1, D_MODEL), lambda i, j: (0, 0))
    return pl.pallas_call(
        _peer_expert_kernel,
        grid=(t // tt, PEER_N_EXPERTS // eb),
        in_specs=[pl.BlockSpec((tt, D_MODEL), lambda i, j: (i, 0)),
                  route(1), route(1), route(2),
                  pl.BlockSpec((eb, D_MODEL), lambda i, j: (j, 0)),
                  pl.BlockSpec((D_MODEL, eb), lambda i, j: (0, j)),
                  vec, vec],
        out_specs=pl.BlockSpec((tt, D_MODEL), lambda i, j: (i, 0)),
        out_shape=jax.ShapeDtypeStruct((t, D_MODEL), F32),
        scratch_shapes=[pltpu.VMEM((D_MODEL, tt), F32), pltpu.VMEM((eb, tt), BF),
                        pltpu.VMEM((tt, D_MODEL), BF),
                        pltpu.VMEM((2, nc * PEER_HEADS * (eb // PEER_N_KEYS), SUBLANES, LANES), F32)],
        compiler_params=_cparams(("parallel", "arbitrary")),
        name="peer_expert_ln",
    )(x2d, e1, cnt, k2, u, vt, g, bb)


def _split_w_in(w):
    sizes = (256, 256, 256, MLA_Q_RANK, MLA_KV_RANK, MLA_ROPE_DIM, 384, 384, 384)
    out, off = [], 0
    for n in sizes:
        out.append(w[:, off:off + n])
        off += n
    return out


def _pad_rows(w, n_heads, d):
    n = w.shape[1]
    w3 = w.reshape(n_heads, d, n)
    return jnp.zeros((n_heads, LANES, n), w.dtype).at[:, :d, :].set(w3).reshape(n_heads * LANES, n)


def _layer(x2d, batch, seq, l, w_in, diff_lambda, diff_subln_g, mla_q_norm_g, mla_kv_norm_g, mla_w_uq,
           mla_w_ukv, w_o, ln1_g, ln1_b, peer_w_q, peer_sub_keys, peer_u, peer_v, ln2_g, ln2_b):
    lambda_init = 0.8 - 0.6 * math.exp(-0.3 * (l + 1))
    a_q, a_k, a_v, b_cq, b_ckv, b_kr, c_q, c_k, c_v = _split_w_in(w_in)
    bf = lambda a: a.astype(BF)

    cq, sq = _rope_tables(seq, DIFF_QK_DIM, 0, LOG2E * DIFF_QK_DIM ** -0.5, 0.0)
    ck, sk = _rope_tables(seq, DIFF_QK_DIM, 0, 1.0, 0.0)
    nh = 2 * DIFF_HEADS
    qa, ka, va = _proj_qkv(
        x2d, seq,
        bf(_pad_heads(a_q, nh, DIFF_QK_DIM)), bf(_pad_heads(_rot_cols(a_q, nh, DIFF_QK_DIM), nh, DIFF_QK_DIM)),
        bf(_pad_heads(a_k, nh, DIFF_QK_DIM)), bf(_pad_heads(_rot_cols(a_k, nh, DIFF_QK_DIM), nh, DIFF_QK_DIM)),
        bf(_pad_heads(a_v, DIFF_HEADS, DIFF_V_DIM)), cq, sq, ck, sk)
    g_pad = jnp.zeros((1, LANES), F32).at[0, :DIFF_V_DIM].set(diff_subln_g.astype(F32))
    sh = lambda a: a.reshape(batch, seq, a.shape[-1])
    out_a = _diff_attn(sh(qa), sh(ka), sh(va), diff_lambda.astype(F32), g_pad, lambda_init)

    qk_dim = MLA_NOPE_DIM + MLA_ROPE_DIM
    cqm, sqm = _rope_tables(seq, MLA_ROPE_DIM, MLA_NOPE_DIM, LOG2E * qk_dim ** -0.5, 1.0)
    cqm = cqm.at[:, qk_dim:].set(0.0)
    ckm, skm = _rope_tables(seq, MLA_ROPE_DIM, MLA_NOPE_DIM, 1.0, 0.0)
    uq3 = mla_w_uq.reshape(MLA_Q_RANK, MLA_HEADS, qk_dim)
    uq_rope = uq3[..., MLA_NOPE_DIM:].reshape(MLA_Q_RANK, MLA_HEADS * MLA_ROPE_DIM)
    uq_rot = _pad_heads(_rot_cols(uq_rope, MLA_HEADS, MLA_ROPE_DIM), MLA_HEADS, MLA_ROPE_DIM, MLA_NOPE_DIM)
    ukv3 = mla_w_ukv.reshape(MLA_KV_RANK, MLA_HEADS, MLA_NOPE_DIM + MLA_V_DIM)
    uk = ukv3[..., :MLA_NOPE_DIM].reshape(MLA_KV_RANK, MLA_HEADS * MLA_NOPE_DIM)
    uv = ukv3[..., MLA_NOPE_DIM:].reshape(MLA_KV_RANK, MLA_HEADS * MLA_V_DIM)
    qb, kb, vb = _proj_mla(
        x2d, seq, bf(b_cq), bf(b_ckv),
        bf(_pad_heads(b_kr, 1, MLA_ROPE_DIM, MLA_NOPE_DIM)),
        bf(_pad_heads(_rot_cols(b_kr, 1, MLA_ROPE_DIM), 1, MLA_ROPE_DIM, MLA_NOPE_DIM)),
        mla_q_norm_g.astype(F32)[None, :], mla_kv_norm_g.astype(F32)[None, :],
        bf(_pad_heads(mla_w_uq, MLA_HEADS, qk_dim)), bf(uq_rot),
        bf(_pad_heads(uk, MLA_HEADS, MLA_NOPE_DIM)), bf(_pad_heads(uv, MLA_HEADS, MLA_V_DIM)),
        cqm, sqm, ckm, skm)
    out_b = _mla_attn(sh(qb), sh(kb), sh(vb))

    cqd, sqd = _rope_tables(seq, DIL_HEAD_DIM, 0, LOG2E * DIL_HEAD_DIM ** -0.5, 0.0)
    ckd, skd = _rope_tables(seq, DIL_HEAD_DIM, 0, 1.0, 0.0)
    nh = DIL_HEADS
    dil_qkv = _proj_dil(
        x2d, batch, seq,
        bf(_pad_heads(c_q, nh, DIL_HEAD_DIM)), bf(_pad_heads(_rot_cols(c_q, nh, DIL_HEAD_DIM), nh, DIL_HEAD_DIM)),
        bf(_pad_heads(c_k, nh, DIL_HEAD_DIM)), bf(_pad_heads(_rot_cols(c_k, nh, DIL_HEAD_DIM), nh, DIL_HEAD_DIM)),
        bf(_pad_heads(c_v, nh, DIL_HEAD_DIM)), cqd, sqd, ckd, skd)
    band = [_band_attn(q, k, v) for q, k, v in dil_qkv]
    fl = lambda a: a.reshape(batch * seq, a.shape[-1])

    wo_a = _pad_rows(w_o[:DIFF_HEADS * DIFF_V_DIM], DIFF_HEADS, DIFF_V_DIM)
    wo_b = _pad_rows(w_o[256:256 + MLA_HEADS * MLA_V_DIM], MLA_HEADS, MLA_V_DIM)
    wo_c = _pad_rows(w_o[640:], DIL_HEADS, DIL_HEAD_DIM)
    x1 = _out_proj(x2d, seq, fl(out_a), fl(out_b), band,
                   bf(wo_a), bf(wo_b), bf(wo_c), ln1_g.astype(F32)[None, :], ln1_b.astype(F32)[None, :])

    keys = bf(peer_sub_keys.reshape(PEER_HEADS * 2, PEER_N_KEYS, PEER_HALF_DIM))
    e1, cnt, k2 = _peer_route(x1, bf(peer_w_q), keys)
    return _peer_expert(x1, e1, cnt, k2, bf(peer_u), bf(peer_v).T,
                        ln2_g.astype(F32)[None, :], ln2_b.astype(F32)[None, :])


def kernel(x, w_in, diff_lambda, diff_subln_g, mla_q_norm_g, mla_kv_norm_g, mla_w_uq, mla_w_ukv, w_o,
           ln1_g, ln1_b, peer_w_q, peer_sub_keys, peer_u, peer_v, ln2_g, ln2_b):
    batch, seq, d = x.shape
    params = (w_in, diff_lambda, diff_subln_g, mla_q_norm_g, mla_kv_norm_g, mla_w_uq, mla_w_ukv, w_o,
              ln1_g, ln1_b, peer_w_q, peer_sub_keys, peer_u, peer_v, ln2_g, ln2_b)
    x2d = x.reshape(batch * seq, d)
    for l in range(DEPTH):
        x2d = _layer(x2d, batch, seq, l, *[p[l] for p in params])
    return x2d.reshape(batch, seq, d)
```

```python
import functools
import math

import jax
import jax.numpy as jnp
from jax import lax
from jax.experimental import pallas as pl
from jax.experimental.pallas import tpu as pltpu

BF = jnp.bfloat16
F32 = jnp.float32

LANES = 128
SUBLANES = 8
D_MODEL = 1024
DEPTH = 2
DIFF_HEADS = 4
DIFF_QK_DIM = 32
DIFF_V_DIM = 64
MLA_HEADS = 6
MLA_Q_RANK = 256
MLA_KV_RANK = 128
MLA_NOPE_DIM = 64
MLA_ROPE_DIM = 32
MLA_V_DIM = 64
DIL_HEADS = 6
DIL_HEAD_DIM = 64
DIL_PAIRS = ((128, 1), (512, 4), (2048, 16))
ROPE_THETA = 10000.0
PEER_HEADS = 8
PEER_N_KEYS = 128
PEER_N_EXPERTS = PEER_N_KEYS * PEER_N_KEYS
PEER_HALF_DIM = 128
PEER_TOPK = 16
DEEPNORM_ALPHA = (2 * DEPTH) ** 0.25
LN_EPS = 1e-5
NEG_INF = -1e30
LOG2E = math.log2(math.e)
LN2 = math.log(2.0)
SQRT_HALF = 0.5 ** 0.5
NO_SECOND_KEY = 2.0

VMEM_LIMIT = 56 * 1024 * 1024

TM_PROJ = 512
TQ_DENSE = 256
TQ_BAND = 256
BAND_HALF = 64
TT_PEER = 512
EB_PEER = 1024
ROW_GROUP = 2


def _cparams(sem):
    return pltpu.CompilerParams(dimension_semantics=sem, vmem_limit_bytes=VMEM_LIMIT)


def _dot(a, b):
    return jnp.dot(a, b, preferred_element_type=F32)


def _dot_nt(a, b):
    return lax.dot_general(a, b, (((1,), (1,)), ((), ())), preferred_element_type=F32)


def _pad_heads(w, n_heads, d, offset=0):
    k = w.shape[0]
    w3 = w.reshape(k, n_heads, d)
    out = jnp.zeros((k, n_heads, LANES), w.dtype).at[:, :, offset:offset + d].set(w3)
    return out.reshape(k, n_heads * LANES)


def _rot_cols(w, n_heads, d):
    k = w.shape[0]
    w3 = w.reshape(k, n_heads, d)
    half = d // 2
    return jnp.concatenate([-w3[..., half:], w3[..., :half]], axis=-1).reshape(k, n_heads * d)


def _rope_tables(seq, d, offset, scale, fill):
    half = d // 2
    inv_freq = jnp.exp(-math.log(ROPE_THETA) * jnp.arange(half, dtype=F32) / half)
    ang = jnp.arange(seq, dtype=jnp.int32).astype(F32)[:, None] * inv_freq[None, :]
    cos, sin = jnp.cos(ang), jnp.sin(ang)
    cos_t = jnp.full((seq, LANES), fill, F32).at[:, offset:offset + d].set(jnp.concatenate([cos, cos], -1))
    sin_t = jnp.zeros((seq, LANES), F32).at[:, offset:offset + d].set(jnp.concatenate([sin, sin], -1))
    return cos_t * scale, sin_t * scale


def _rope_store(o_ref, h, hr, cos, sin):
    for s in range(o_ref.shape[1] // LANES):
        sl = slice(s * LANES, (s + 1) * LANES)
        o_ref[:, sl] = (h[:, sl] * cos + hr[:, sl] * sin).astype(o_ref.dtype)


def _proj_qkv_kernel(x_ref, wq_ref, wqr_ref, wk_ref, wkr_ref, wv_ref, cq_ref, sq_ref, ck_ref, sk_ref,
                     q_ref, k_ref, v_ref):
    xb = x_ref[...].astype(BF)
    _rope_store(q_ref, _dot(xb, wq_ref[...]), _dot(xb, wqr_ref[...]), cq_ref[...], sq_ref[...])
    _rope_store(k_ref, _dot(xb, wk_ref[...]), _dot(xb, wkr_ref[...]), ck_ref[...], sk_ref[...])
    v_ref[...] = _dot(xb, wv_ref[...]).astype(v_ref.dtype)


def _proj_qkv(x2d, seq, wq, wqr, wk, wkr, wv, cq, sq, ck, sk):
    t = x2d.shape[0]
    tm = TM_PROJ
    nq, nv = wq.shape[1], wv.shape[1]
    pos_blocks = seq // tm
    full = lambda a: pl.BlockSpec(a.shape, lambda i: (0, 0))
    tab = pl.BlockSpec((tm, LANES), lambda i: (i % pos_blocks, 0))
    return pl.pallas_call(
        _proj_qkv_kernel,
        grid=(t // tm,),
        in_specs=[pl.BlockSpec((tm, D_MODEL), lambda i: (i, 0)),
                  full(wq), full(wqr), full(wk), full(wkr), full(wv), tab, tab, tab, tab],
        out_specs=[pl.BlockSpec((tm, nq), lambda i: (i, 0)),
                   pl.BlockSpec((tm, nq), lambda i: (i, 0)),
                   pl.BlockSpec((tm, nv), lambda i: (i, 0))],
        out_shape=[jax.ShapeDtypeStruct((t, nq), BF), jax.ShapeDtypeStruct((t, nq), BF),
                   jax.ShapeDtypeStruct((t, nv), BF)],
        compiler_params=_cparams(("parallel",)),
        name="proj_qkv",
    )(x2d, wq, wqr, wk, wkr, wv, cq, sq, ck, sk)


def _proj_dil_kernel(x_ref, wq_ref, wqr_ref, wk_ref, wkr_ref, wv_ref, cq_ref, sq_ref, ck_ref, sk_ref,
                     *refs):
    outs, scr = refs[:-1], refs[-1]
    n_slabs = scr.shape[0]
    xb = x_ref[...].astype(BF)

    def rope(h, hr, cos, sin):
        return jnp.concatenate(
            [h[:, s * LANES:(s + 1) * LANES] * cos + hr[:, s * LANES:(s + 1) * LANES] * sin
             for s in range(n_slabs)], axis=-1)

    def emit(h, which):
        for s in range(n_slabs):
            scr[s] = h[:, s * LANES:(s + 1) * LANES]
        for p in range(len(outs) // 3):
            o_ref = outs[3 * p + which]
            dil, rows, _ = o_ref.shape
            for c in range(dil):
                for s in range(n_slabs):
                    o_ref[c, :, s * LANES:(s + 1) * LANES] = (
                        scr[s, pl.ds(c, rows, stride=dil), :].astype(o_ref.dtype))

    emit(rope(_dot(xb, wq_ref[...]), _dot(xb, wqr_ref[...]), cq_ref[...], sq_ref[...]), 0)
    emit(rope(_dot(xb, wk_ref[...]), _dot(xb, wkr_ref[...]), ck_ref[...], sk_ref[...]), 1)
    emit(_dot(xb, wv_ref[...]), 2)


def _proj_dil(x2d, batch, seq, wq, wqr, wk, wkr, wv, cq, sq, ck, sk):
    t = x2d.shape[0]
    tm = TM_PROJ
    w = wq.shape[1]
    pos_blocks = seq // tm
    full = lambda a: pl.BlockSpec(a.shape, lambda i: (0, 0))
    tab = pl.BlockSpec((tm, LANES), lambda i: (i % pos_blocks, 0))
    out_specs, out_shape = [], []
    for _, dil in DIL_PAIRS:
        for _ in range(3):
            out_specs.append(pl.BlockSpec((None, dil, tm // dil, w),
                                          lambda i: (i // pos_blocks, 0, i % pos_blocks, 0)))
            out_shape.append(jax.ShapeDtypeStruct((batch, dil, seq // dil, w), BF))
    outs = pl.pallas_call(
        _proj_dil_kernel,
        grid=(t // tm,),
        in_specs=[pl.BlockSpec((tm, D_MODEL), lambda i: (i, 0)),
                  full(wq), full(wqr), full(wk), full(wkr), full(wv), tab, tab, tab, tab],
        out_specs=out_specs,
        out_shape=out_shape,
        scratch_shapes=[pltpu.VMEM((w // LANES, tm, LANES), F32)],
        compiler_params=_cparams(("parallel",)),
        name="proj_dil",
    )(x2d, wq, wqr, wk, wkr, wv, cq, sq, ck, sk)
    return [outs[3 * p:3 * p + 3] for p in range(len(DIL_PAIRS))]


def _rms(x, g, n):
    return x * lax.rsqrt(jnp.sum(x * x, axis=-1, keepdims=True) / n + LN_EPS) * g


def _proj_mla_kernel(x_ref, wcq_ref, wckv_ref, wkr_ref, wkrr_ref, gq_ref, gkv_ref,
                     wuq_ref, wuqr_ref, wuk_ref, wuv_ref, cq_ref, sq_ref, ck_ref, sk_ref,
                     q_ref, k_ref, v_ref):
    xb = x_ref[...].astype(BF)
    nq = _rms(_dot(xb, wcq_ref[...]), gq_ref[...], MLA_Q_RANK).astype(BF)
    nkv = _rms(_dot(xb, wckv_ref[...]), gkv_ref[...], MLA_KV_RANK).astype(BF)
    _rope_store(q_ref, _dot(nq, wuq_ref[...]), _dot(nq, wuqr_ref[...]), cq_ref[...], sq_ref[...])
    kpe = _dot(xb, wkr_ref[...]) * ck_ref[...] + _dot(xb, wkrr_ref[...]) * sk_ref[...]
    kn = _dot(nkv, wuk_ref[...])
    for s in range(MLA_HEADS):
        sl = slice(s * LANES, (s + 1) * LANES)
        k_ref[:, sl] = (kn[:, sl] + kpe).astype(k_ref.dtype)
    v_ref[...] = _dot(nkv, wuv_ref[...]).astype(v_ref.dtype)


def _proj_mla(x2d, seq, wcq, wckv, wkr, wkrr, gq, gkv, wuq, wuqr, wuk, wuv, cq, sq, ck, sk):
    t = x2d.shape[0]
    tm = TM_PROJ
    n = MLA_HEADS * LANES
    pos_blocks = seq // tm
    full = lambda a: pl.BlockSpec(a.shape, lambda i: (0, 0))
    tab = pl.BlockSpec((tm, LANES), lambda i: (i % pos_blocks, 0))
    out = pl.BlockSpec((tm, n), lambda i: (i, 0))
    return pl.pallas_call(
        _proj_mla_kernel,
        grid=(t // tm,),
        in_specs=[pl.BlockSpec((tm, D_MODEL), lambda i: (i, 0)),
                  full(wcq), full(wckv), full(wkr), full(wkrr), full(gq), full(gkv),
                  full(wuq), full(wuqr), full(wuk), full(wuv), tab, tab, tab, tab],
        out_specs=[out, out, out],
        out_shape=[jax.ShapeDtypeStruct((t, n), BF)] * 3,
        compiler_params=_cparams(("parallel",)),
        name="proj_mla",
    )(x2d, wcq, wckv, wkr, wkrr, gq, gkv, wuq, wuqr, wuk, wuv, cq, sq, ck, sk)


def _softmax_pv(q, k, v):
    s = _dot_nt(q, k)
    m = jnp.max(s, axis=-1, keepdims=True)
    p = jnp.exp2(s - m)
    l = jnp.sum(p, axis=-1, keepdims=True)
    return _dot(p.astype(BF), v) / l


def _diff_attn_kernel(lam_ref, g_ref, q1_ref, q2_ref, k1_ref, k2_ref, v_ref, o_ref, *, lambda_init):
    lam = lam_ref[...]
    lam_full = (jnp.exp(jnp.sum(lam[0:1] * lam[1:2], axis=-1, keepdims=True))
                - jnp.exp(jnp.sum(lam[2:3] * lam[3:4], axis=-1, keepdims=True)) + lambda_init)
    v = v_ref[0]
    o1 = _softmax_pv(q1_ref[0], k1_ref[0], v)
    o2 = _softmax_pv(q2_ref[0], k2_ref[0], v)
    o = o1 - lam_full * o2
    o_ref[0] = (_rms(o, g_ref[...], DIFF_V_DIM) * (1.0 - lambda_init)).astype(o_ref.dtype)


def _diff_attn(q, k, v, lam, g_pad, lambda_init):
    b, s, _ = q.shape
    tq = TQ_DENSE
    qspec = lambda off: pl.BlockSpec((1, tq, LANES), lambda bi, h, i: (bi, i, 2 * h + off))
    kspec = lambda off: pl.BlockSpec((1, s, LANES), lambda bi, h, i: (bi, 0, 2 * h + off))
    return pl.pallas_call(
        functools.partial(_diff_attn_kernel, lambda_init=lambda_init),
        grid=(b, DIFF_HEADS, s // tq),
        in_specs=[pl.BlockSpec(lam.shape, lambda bi, h, i: (0, 0)),
                  pl.BlockSpec(g_pad.shape, lambda bi, h, i: (0, 0)),
                  qspec(0), qspec(1), kspec(0), kspec(1),
                  pl.BlockSpec((1, s, LANES), lambda bi, h, i: (bi, 0, h))],
        out_specs=pl.BlockSpec((1, tq, LANES), lambda bi, h, i: (bi, i, h)),
        out_shape=jax.ShapeDtypeStruct((b, s, DIFF_HEADS * LANES), BF),
        compiler_params=_cparams(("parallel", "parallel", "arbitrary")),
        name="diff_attn",
    )(lam, g_pad, q, q, k, k, v)


def _mla_attn_kernel(q_ref, k_ref, v_ref, o_ref):
    o_ref[0] = _softmax_pv(q_ref[0], k_ref[0], v_ref[0]).astype(o_ref.dtype)


def _mla_attn(q, k, v):
    b, s, _ = q.shape
    tq = TQ_DENSE
    kv = pl.BlockSpec((1, s, LANES), lambda bi, h, i: (bi, 0, h))
    qo = pl.BlockSpec((1, tq, LANES), lambda bi, h, i: (bi, i, h))
    return pl.pallas_call(
        _mla_attn_kernel,
        grid=(b, MLA_HEADS, s // tq),
        in_specs=[qo, kv, kv],
        out_specs=qo,
        out_shape=jax.ShapeDtypeStruct((b, s, MLA_HEADS * LANES), BF),
        compiler_params=_cparams(("parallel", "parallel", "arbitrary")),
        name="mla_attn",
    )(q, k, v)


def _band_attn_kernel(q_ref, kp_ref, kc_ref, kn_ref, vp_ref, vc_ref, vn_ref, o_ref, lse_ref, *, seg_len):
    i = pl.program_id(2)
    tq = q_ref.shape[0]
    blk = LANES
    span = blk + 2 * BAND_HALF
    kk = jnp.concatenate([kp_ref[...], kc_ref[...], kn_ref[...]], axis=0)
    vv = jnp.concatenate([vp_ref[...], vc_ref[...], vn_ref[...]], axis=0)
    qi = lax.broadcasted_iota(jnp.int32, (blk, span), 0)
    kj = lax.broadcasted_iota(jnp.int32, (blk, span), 1)
    for j in range(tq // blk):
        qpos = i * tq + j * blk + qi
        kpos = i * tq + j * blk - BAND_HALF + kj
        mask = (jnp.abs(kpos - qpos) <= BAND_HALF) & (kpos >= 0) & (kpos < seg_len)
        for h in range(DIL_HEADS):
            cs = slice(h * LANES, (h + 1) * LANES)
            rows = slice(j * blk, (j + 1) * blk)
            k = kk[j * blk:j * blk + span, cs]
            v = vv[j * blk:j * blk + span, cs]
            s = jnp.where(mask, _dot_nt(q_ref[rows, cs], k), NEG_INF)
            m = jnp.max(s, axis=-1, keepdims=True)
            p = jnp.exp2(s - m)
            l = jnp.sum(p, axis=-1, keepdims=True)
            o_ref[rows, cs] = _dot(p.astype(BF), v) / l
            lse_ref[rows, cs] = jnp.broadcast_to(m * LN2 + jnp.log(l), (blk, LANES))


def _band_attn(q, k, v):
    b, dil, seg, w = q.shape
    tq = TQ_BAND
    r = tq // BAND_HALF
    nb = seg // BAND_HALF
    cur = pl.BlockSpec((None, None, tq, w), lambda bi, c, i: (bi, c, i, 0))
    prev = pl.BlockSpec((None, None, BAND_HALF, w), lambda bi, c, i: (bi, c, jnp.maximum(i * r - 1, 0), 0))
    nxt = pl.BlockSpec((None, None, BAND_HALF, w),
                       lambda bi, c, i: (bi, c, jnp.minimum((i + 1) * r, nb - 1), 0))
    return pl.pallas_call(
        functools.partial(_band_attn_kernel, seg_len=seg),
        grid=(b, dil, seg // tq),
        in_specs=[cur, prev, cur, nxt, prev, cur, nxt],
        out_specs=[cur, cur],
        out_shape=[jax.ShapeDtypeStruct((b, dil, seg, w), F32)] * 2,
        compiler_params=_cparams(("parallel", "parallel", "arbitrary")),
        name="band_attn",
    )(q, k, k, k, v, v, v)


def _layer_norm(y, g, b):
    mu = jnp.mean(y, axis=-1, keepdims=True)
    yc = y - mu
    var = jnp.mean(yc * yc, axis=-1, keepdims=True)
    return yc * lax.rsqrt(var + LN_EPS) * g + b


def _to_token_order(src_ref, dst_ref):
    dil, rows, w = src_ref.shape
    for c in range(dil):
        for s in range(w // LANES):
            dst_ref[s, pl.ds(c, rows, stride=dil), :] = src_ref[c, :, s * LANES:(s + 1) * LANES]


def _out_proj_kernel(x_ref, oa_ref, ob_ref, o1_ref, o2_ref, o3_ref, l1_ref, l2_ref, l3_ref,
                     wa_ref, wb_ref, wc_ref, g_ref, b_ref, y_ref, o2_scr, o3_scr, l2_scr, l3_scr):
    _to_token_order(o2_ref, o2_scr)
    _to_token_order(o3_ref, o3_scr)
    _to_token_order(l2_ref, l2_scr)
    _to_token_order(l3_ref, l3_scr)
    heads = []
    for s in range(DIL_HEADS):
        sl = slice(s * LANES, (s + 1) * LANES)
        l1, l2, l3 = l1_ref[0, :, sl], l2_scr[s], l3_scr[s]
        m = jnp.maximum(jnp.maximum(l1, l2), l3)
        e1, e2, e3 = jnp.exp(l1 - m), jnp.exp(l2 - m), jnp.exp(l3 - m)
        oc = (e1 * o1_ref[0, :, sl] + e2 * o2_scr[s] + e3 * o3_scr[s]) / (e1 + e2 + e3)
        heads.append(oc.astype(BF))
    mix = (_dot(oa_ref[...], wa_ref[...]) + _dot(ob_ref[...], wb_ref[...])
           + _dot(jnp.concatenate(heads, axis=-1), wc_ref[...]))
    y_ref[...] = _layer_norm(DEEPNORM_ALPHA * x_ref[...] + mix, g_ref[...], b_ref[...])


def _out_proj(x2d, seq, oa, ob, band, wa, wb, wc, g, bb):
    t = x2d.shape[0]
    tm = TM_PROJ
    pos_blocks = seq // tm
    w = DIL_HEADS * LANES
    row = lambda a: pl.BlockSpec((tm, a.shape[1]), lambda i: (i, 0))
    full = lambda a: pl.BlockSpec(a.shape, lambda i: (0, 0))
    cls = lambda a: pl.BlockSpec((None, a.shape[1], tm // a.shape[1], w),
                                 lambda i: (i // pos_blocks, 0, i % pos_blocks, 0))
    (o1, l1), (o2, l2), (o3, l3) = band
    slab = pltpu.VMEM((DIL_HEADS, tm, LANES), F32)
    return pl.pallas_call(
        _out_proj_kernel,
        grid=(t // tm,),
        in_specs=[row(x2d), row(oa), row(ob), cls(o1), cls(o2), cls(o3), cls(l1), cls(l2), cls(l3),
                  full(wa), full(wb), full(wc), full(g), full(bb)],
        out_specs=row(x2d),
        out_shape=jax.ShapeDtypeStruct((t, D_MODEL), F32),
        scratch_shapes=[slab, slab, slab, slab],
        compiler_params=_cparams(("parallel",)),
        name="out_proj_ln",
    )(x2d, oa, ob, o1, o2, o3, l1, l2, l3, wa, wb, wc, g, bb)


def _sort16_network():
    pairs, p = [], 1
    while p < PEER_TOPK:
        k = p
        while k >= 1:
            for j in range(k % p, PEER_TOPK - k, 2 * k):
                for i in range(min(k, PEER_TOPK - j - k)):
                    if (i + j) // (2 * p) == (i + j + k) // (2 * p):
                        pairs.append((i + j, i + j + k))
            k //= 2
        p *= 2
    return pairs


def _compare_swap(v, i, j):
    v[i], v[j] = jnp.maximum(v[i], v[j]), jnp.minimum(v[i], v[j])


def _top16_sorted(s):
    v = [s[i * SUBLANES:(i + 1) * SUBLANES] for i in range(PEER_TOPK)]
    for i, j in _sort16_network():
        _compare_swap(v, i, j)
    shift = SUBLANES // 2
    while shift >= 1:
        other = [pltpu.roll(x, shift, 0) for x in v]
        v = [jnp.maximum(v[i], other[PEER_TOPK - 1 - i]) for i in range(PEER_TOPK)]
        d = PEER_TOPK // 2
        while d >= 1:
            for i in range(PEER_TOPK):
                if i & d == 0:
                    _compare_swap(v, i, i + d)
            d //= 2
        shift //= 2
    return v


def _route_chunk(s1, s2):
    v1 = _top16_sorted(s1)
    v2 = _top16_sorted(s2)
    t1 = jnp.concatenate([x[0:1] for x in v1], axis=0)
    t2 = jnp.concatenate([x[0:1] for x in v2], axis=0)
    cand = jnp.concatenate(
        [t1[0:1] + t2]
        + [t1[k1:k1 + 1] + t2[0:8] for k1 in range(1, 8)]
        + [t1[8:16] + t2[0:1]], axis=0)
    c0 = t1[0:1] + t2[0:1]
    work, z, tau = cand, jnp.zeros_like(c0), c0
    for _ in range(PEER_TOPK):
        tau = jnp.max(work, axis=0, keepdims=True)
        work = jnp.where(work == tau, -jnp.inf, work)
        z = z + jnp.exp(tau - c0)
    cnt_of_rank = jnp.zeros_like(t1)
    for k2 in range(PEER_TOPK):
        cnt_of_rank = cnt_of_rank + jnp.where(t1 + t2[k2:k2 + 1] >= tau, 1.0, 0.0)
    e2_top = jnp.exp(t2 - t2[0:1])
    thr_of_rank = jnp.full_like(t1, NO_SECOND_KEY)
    for c in range(1, PEER_TOPK + 1):
        thr_of_rank = jnp.where(cnt_of_rank == float(c), e2_top[c - 1:c], thr_of_rank)
    thr = [jnp.full((SUBLANES, s1.shape[1]), NO_SECOND_KEY, F32)] * PEER_TOPK
    for k1 in range(PEER_TOPK):
        row = jnp.broadcast_to(thr_of_rank[k1:k1 + 1], (SUBLANES, s1.shape[1]))
        thr = [jnp.where(s1[i * SUBLANES:(i + 1) * SUBLANES] == v1[k1], row, thr[i])
               for i in range(PEER_TOPK)]
    e1 = jnp.exp(s1 - t1[0:1]) / z
    e2 = jnp.exp(s2 - t2[0:1])
    return e1, jnp.concatenate(thr, axis=0), e2


def _peer_route_kernel(x_ref, wq_ref, keys_ref, e1_ref, thr_ref, e2_ref, q_scr, s_scr):
    nk = PEER_N_KEYS
    n_chunks = e1_ref.shape[0]
    q = _dot(x_ref[...].astype(BF), wq_ref[...]).astype(BF)
    for s in range(2 * PEER_HEADS):
        q_scr[s] = q[:, s * nk:(s + 1) * nk]

    def head_body(h, carry):
        s1 = _dot_nt(keys_ref[2 * h], q_scr[2 * h])
        s2 = _dot_nt(keys_ref[2 * h + 1], q_scr[2 * h + 1])
        for c in range(n_chunks):
            s_scr[0, c] = s1[:, c * LANES:(c + 1) * LANES]
            s_scr[1, c] = s2[:, c * LANES:(c + 1) * LANES]
        rows = pl.ds(pl.multiple_of(h * nk, nk), nk)

        def chunk_body(c, carry2):
            e1, thr, e2 = _route_chunk(s_scr[0, c], s_scr[1, c])
            e1_ref[c, rows, :] = e1
            thr_ref[c, rows, :] = thr
            e2_ref[c, rows, :] = e2
            return carry2

        lax.fori_loop(0, n_chunks, chunk_body, 0)
        return carry

    lax.fori_loop(0, PEER_HEADS, head_body, 0)


def _peer_route(x2d, wq, keys):
    t = x2d.shape[0]
    tt = TT_PEER
    nc = tt // LANES
    rows = PEER_HEADS * PEER_N_KEYS
    out = pl.BlockSpec((nc, rows, LANES), lambda i: (i, 0, 0))
    shape = jax.ShapeDtypeStruct((t // LANES, rows, LANES), F32)
    return pl.pallas_call(
        _peer_route_kernel,
        grid=(t // tt,),
        in_specs=[pl.BlockSpec((tt, D_MODEL), lambda i: (i, 0)),
                  pl.BlockSpec(wq.shape, lambda i: (0, 0)),
                  pl.BlockSpec(keys.shape, lambda i: (0, 0, 0))],
        out_specs=[out] * 3,
        out_shape=[shape] * 3,
        scratch_shapes=[pltpu.VMEM((2 * PEER_HEADS, tt, PEER_N_KEYS), BF),
                        pltpu.VMEM((2, nc, PEER_N_KEYS, LANES), F32)],
        compiler_params=_cparams(("parallel",)),
        name="peer_route",
    )(x2d, wq, keys)


def _peer_expert_kernel(x_ref, e1_ref, thr_ref, e2_ref, u_ref, vt_ref, g_ref, b_ref,
                        y_ref, acc_ref, a_ref, xb_ref, row_ref):
    j = pl.program_id(1)
    nk = PEER_N_KEYS
    rows_per_step = u_ref.shape[0] // nk
    n_chunks = e1_ref.shape[0]

    @pl.when(j == 0)
    def _():
        acc_ref[...] = jnp.zeros_like(acc_ref)
        xb_ref[...] = (x_ref[...] * SQRT_HALF).astype(BF)

    n_tiles = nk // SUBLANES
    for c in range(n_chunks):
        for h in range(PEER_HEADS):
            rows = pl.ds(pl.multiple_of(h * nk + j * rows_per_step, rows_per_step), rows_per_step)
            e1_rows = e1_ref[c, rows, :] * SQRT_HALF
            thr_rows = thr_ref[c, rows, :]
            for r in range(rows_per_step):
                idx = (c * PEER_HEADS + h) * rows_per_step + r
                row_ref[0, idx] = jnp.broadcast_to(e1_rows[r:r + 1], (SUBLANES, LANES))
                row_ref[1, idx] = jnp.broadcast_to(thr_rows[r:r + 1], (SUBLANES, LANES))

    ht = _dot_nt(u_ref[...], xb_ref[...])
    for c in range(n_chunks):
        cs = slice(c * LANES, (c + 1) * LANES)
        for r0 in range(0, rows_per_step, ROW_GROUP):
            w = [[jnp.zeros((SUBLANES, LANES), F32)] * n_tiles for _ in range(ROW_GROUP)]
            for h in range(PEER_HEADS):
                idx = (c * PEER_HEADS + h) * rows_per_step + r0
                e1 = [row_ref[0, idx + r] for r in range(ROW_GROUP)]
                thr = [row_ref[1, idx + r] for r in range(ROW_GROUP)]
                for k in range(n_tiles):
                    first = (h * n_tiles + k) * SUBLANES
                    e2 = e2_ref[c, first:first + SUBLANES, :]
                    for r in range(ROW_GROUP):
                        w[r][k] = w[r][k] + jnp.where(e2 >= thr[r], e1[r] * e2, 0.0)
            for r in range(ROW_GROUP):
                for k in range(0, n_tiles, 2):
                    bs = slice((r0 + r) * nk + k * SUBLANES, (r0 + r) * nk + (k + 2) * SUBLANES)
                    gate = jnp.concatenate([w[r][k], w[r][k + 1]], axis=0)
                    hb = ht[bs, cs]
                    a_ref[bs, cs] = (gate * (hb * (1.0 + lax.erf(hb)))).astype(BF)
    acc_ref[...] += _dot(vt_ref[...], a_ref[...])

    @pl.when(j == pl.num_programs(1) - 1)
    def _():
        ffn = acc_ref[...].T
        y_ref[...] = _layer_norm(DEEPNORM_ALPHA * x_ref[...] + ffn, g_ref[...], b_ref[...])


def _peer_expert(x2d, e1, thr, e2, u, vt, g, bb):
    t = x2d.shape[0]
    tt, eb = TT_PEER, EB_PEER
    nc = tt // LANES
    rows = PEER_HEADS * PEER_N_KEYS
    route = pl.BlockSpec((nc, rows, LANES), lambda i, j: (i, 0, 0))
    vec = pl.BlockSpec((1, D_MODEL), lambda i, j: (0, 0))
    return pl.pallas_call(
        _peer_expert_kernel,
        grid=(t // tt, PEER_N_EXPERTS // eb),
        in_specs=[pl.BlockSpec((tt, D_MODEL), lambda i, j: (i, 0)),
                  route, route, route,
                  pl.BlockSpec((eb, D_MODEL), lambda i, j: (j, 0)),
                  pl.BlockSpec((D_MODEL, eb), lambda i, j: (0, j)),
                  vec, vec],
        out_specs=pl.BlockSpec((tt, D_MODEL), lambda i, j: (i, 0)),
        out_shape=jax.ShapeDtypeStruct((t, D_MODEL), F32),
        scratch_shapes=[pltpu.VMEM((D_MODEL, tt), F32), pltpu.VMEM((eb, tt), BF),
                        pltpu.VMEM((tt, D_MODEL), BF),
                        pltpu.VMEM((2, nc * PEER_HEADS * (eb // PEER_N_KEYS), SUBLANES, LANES), F32)],
        compiler_params=_cparams(("parallel", "arbitrary")),
        name="peer_expert_ln",
    )(x2d, e1, thr, e2, u, vt, g, bb)


def _split_w_in(w):
    sizes = (256, 256, 256, MLA_Q_RANK, MLA_KV_RANK, MLA_ROPE_DIM, 384, 384, 384)
    out, off = [], 0
    for n in sizes:
        out.append(w[:, off:off + n])
        off += n
    return out


def _pad_rows(w, n_heads, d):
    n = w.shape[1]
    w3 = w.reshape(n_heads, d, n)
    return jnp.zeros((n_heads, LANES, n), w.dtype).at[:, :d, :].set(w3).reshape(n_heads * LANES, n)


def _layer(x2d, batch, seq, l, w_in, diff_lambda, diff_subln_g, mla_q_norm_g, mla_kv_norm_g, mla_w_uq,
           mla_w_ukv, w_o, ln1_g, ln1_b, peer_w_q, peer_sub_keys, peer_u, peer_v, ln2_g, ln2_b):
    lambda_init = 0.8 - 0.6 * math.exp(-0.3 * (l + 1))
    a_q, a_k, a_v, b_cq, b_ckv, b_kr, c_q, c_k, c_v = _split_w_in(w_in)
    bf = lambda a: a.astype(BF)

    cq, sq = _rope_tables(seq, DIFF_QK_DIM, 0, LOG2E * DIFF_QK_DIM ** -0.5, 0.0)
    ck, sk = _rope_tables(seq, DIFF_QK_DIM, 0, 1.0, 0.0)
    nh = 2 * DIFF_HEADS
    qa, ka, va = _proj_qkv(
        x2d, seq,
        bf(_pad_heads(a_q, nh, DIFF_QK_DIM)), bf(_pad_heads(_rot_cols(a_q, nh, DIFF_QK_DIM), nh, DIFF_QK_DIM)),
        bf(_pad_heads(a_k, nh, DIFF_QK_DIM)), bf(_pad_heads(_rot_cols(a_k, nh, DIFF_QK_DIM), nh, DIFF_QK_DIM)),
        bf(_pad_heads(a_v, DIFF_HEADS, DIFF_V_DIM)), cq, sq, ck, sk)
    g_pad = jnp.zeros((1, LANES), F32).at[0, :DIFF_V_DIM].set(diff_subln_g.astype(F32))
    sh = lambda a: a.reshape(batch, seq, a.shape[-1])
    out_a = _diff_attn(sh(qa), sh(ka), sh(va), diff_lambda.astype(F32), g_pad, lambda_init)

    qk_dim = MLA_NOPE_DIM + MLA_ROPE_DIM
    cqm, sqm = _rope_tables(seq, MLA_ROPE_DIM, MLA_NOPE_DIM, LOG2E * qk_dim ** -0.5, 1.0)
    cqm = cqm.at[:, qk_dim:].set(0.0)
    ckm, skm = _rope_tables(seq, MLA_ROPE_DIM, MLA_NOPE_DIM, 1.0, 0.0)
    uq3 = mla_w_uq.reshape(MLA_Q_RANK, MLA_HEADS, qk_dim)
    uq_rope = uq3[..., MLA_NOPE_DIM:].reshape(MLA_Q_RANK, MLA_HEADS * MLA_ROPE_DIM)
    uq_rot = _pad_heads(_rot_cols(uq_rope, MLA_HEADS, MLA_ROPE_DIM), MLA_HEADS, MLA_ROPE_DIM, MLA_NOPE_DIM)
    ukv3 = mla_w_ukv.reshape(MLA_KV_RANK, MLA_HEADS, MLA_NOPE_DIM + MLA_V_DIM)
    uk = ukv3[..., :MLA_NOPE_DIM].reshape(MLA_KV_RANK, MLA_HEADS * MLA_NOPE_DIM)
    uv = ukv3[..., MLA_NOPE_DIM:].reshape(MLA_KV_RANK, MLA_HEADS * MLA_V_DIM)
    qb, kb, vb = _proj_mla(
        x2d, seq, bf(b_cq), bf(b_ckv),
        bf(_pad_heads(b_kr, 1, MLA_ROPE_DIM, MLA_NOPE_DIM)),
        bf(_pad_heads(_rot_cols(b_kr, 1, MLA_ROPE_DIM), 1, MLA_ROPE_DIM, MLA_NOPE_DIM)),
        mla_q_norm_g.astype(F32)[None, :], mla_kv_norm_g.astype(F32)[None, :],
        bf(_pad_heads(mla_w_uq, MLA_HEADS, qk_dim)), bf(uq_rot),
        bf(_pad_heads(uk, MLA_HEADS, MLA_NOPE_DIM)), bf(_pad_heads(uv, MLA_HEADS, MLA_V_DIM)),
        cqm, sqm, ckm, skm)
    out_b = _mla_attn(sh(qb), sh(kb), sh(vb))

    cqd, sqd = _rope_tables(seq, DIL_HEAD_DIM, 0, LOG2E * DIL_HEAD_DIM ** -0.5, 0.0)
    ckd, skd = _rope_tables(seq, DIL_HEAD_DIM, 0, 1.0, 0.0)
    nh = DIL_HEADS
    dil_qkv = _proj_dil(
        x2d, batch, seq,
        bf(_pad_heads(c_q, nh, DIL_HEAD_DIM)), bf(_pad_heads(_rot_cols(c_q, nh, DIL_HEAD_DIM), nh, DIL_HEAD_DIM)),
        bf(_pad_heads(c_k, nh, DIL_HEAD_DIM)), bf(_pad_heads(_rot_cols(c_k, nh, DIL_HEAD_DIM), nh, DIL_HEAD_DIM)),
        bf(_pad_heads(c_v, nh, DIL_HEAD_DIM)), cqd, sqd, ckd, skd)
    band = [_band_attn(q, k, v) for q, k, v in dil_qkv]
    fl = lambda a: a.reshape(batch * seq, a.shape[-1])

    wo_a = _pad_rows(w_o[:DIFF_HEADS * DIFF_V_DIM], DIFF_HEADS, DIFF_V_DIM)
    wo_b = _pad_rows(w_o[256:256 + MLA_HEADS * MLA_V_DIM], MLA_HEADS, MLA_V_DIM)
    wo_c = _pad_rows(w_o[640:], DIL_HEADS, DIL_HEAD_DIM)
    x1 = _out_proj(x2d, seq, fl(out_a), fl(out_b), band,
                   bf(wo_a), bf(wo_b), bf(wo_c), ln1_g.astype(F32)[None, :], ln1_b.astype(F32)[None, :])

    keys = bf(peer_sub_keys.reshape(PEER_HEADS * 2, PEER_N_KEYS, PEER_HALF_DIM))
    e1, thr, e2 = _peer_route(x1, bf(peer_w_q), keys)
    return _peer_expert(x1, e1, thr, e2, bf(peer_u), bf(peer_v).T,
                        ln2_g.astype(F32)[None, :], ln2_b.astype(F32)[None, :])


def kernel(x, w_in, diff_lambda, diff_subln_g, mla_q_norm_g, mla_kv_norm_g, mla_w_uq, mla_w_ukv, w_o,
           ln1_g, ln1_b, peer_w_q, peer_sub_keys, peer_u, peer_v, ln2_g, ln2_b):
    batch, seq, d = x.shape
    params = (w_in, diff_lambda, diff_subln_g, mla_q_norm_g, mla_kv_norm_g, mla_w_uq, mla_w_ukv, w_o,
              ln1_g, ln1_b, peer_w_q, peer_sub_keys, peer_u, peer_v, ln2_g, ln2_b)
    x2d = x.reshape(batch * seq, d)
    for l in range(DEPTH):
        x2d = _layer(x2d, batch, seq, l, *[p[l] for p in params])
    return x2d.reshape(batch, seq, d)
```

```python
import functools
import math

import jax
import jax.numpy as jnp
from jax import lax
from jax.experimental import pallas as pl
from jax.experimental.pallas import tpu as pltpu

BF = jnp.bfloat16
F32 = jnp.float32

LANES = 128
SUBLANES = 8
D_MODEL = 1024
DEPTH = 2
DIFF_HEADS = 4
DIFF_QK_DIM = 32
DIFF_V_DIM = 64
MLA_HEADS = 6
MLA_Q_RANK = 256
MLA_KV_RANK = 128
MLA_NOPE_DIM = 64
MLA_ROPE_DIM = 32
MLA_V_DIM = 64
DIL_HEADS = 6
DIL_HEAD_DIM = 64
DIL_PAIRS = ((128, 1), (512, 4), (2048, 16))
ROPE_THETA = 10000.0
PEER_HEADS = 8
PEER_N_KEYS = 128
PEER_N_EXPERTS = PEER_N_KEYS * PEER_N_KEYS
PEER_HALF_DIM = 128
PEER_TOPK = 16
DEEPNORM_ALPHA = (2 * DEPTH) ** 0.25
LN_EPS = 1e-5
NEG_INF = -1e30
LOG2E = math.log2(math.e)
LN2 = math.log(2.0)
SQRT_HALF = 0.5 ** 0.5
NO_SECOND_KEY = 2.0

VMEM_LIMIT = 56 * 1024 * 1024

TM_PROJ = 512
TQ_DENSE = 256
TQ_BAND = 256
BAND_HALF = 64
TT_PEER = 512
EB_PEER = 2048
ROW_GROUP = 2


def _cparams(sem):
    return pltpu.CompilerParams(dimension_semantics=sem, vmem_limit_bytes=VMEM_LIMIT)


def _dot(a, b):
    return jnp.dot(a, b, preferred_element_type=F32)


def _dot_nt(a, b):
    return lax.dot_general(a, b, (((1,), (1,)), ((), ())), preferred_element_type=F32)


def _pad_heads(w, n_heads, d, offset=0):
    k = w.shape[0]
    w3 = w.reshape(k, n_heads, d)
    out = jnp.zeros((k, n_heads, LANES), w.dtype).at[:, :, offset:offset + d].set(w3)
    return out.reshape(k, n_heads * LANES)


def _rot_cols(w, n_heads, d):
    k = w.shape[0]
    w3 = w.reshape(k, n_heads, d)
    half = d // 2
    return jnp.concatenate([-w3[..., half:], w3[..., :half]], axis=-1).reshape(k, n_heads * d)


def _rope_tables(seq, d, offset, scale, fill):
    half = d // 2
    inv_freq = jnp.exp(-math.log(ROPE_THETA) * jnp.arange(half, dtype=F32) / half)
    ang = jnp.arange(seq, dtype=jnp.int32).astype(F32)[:, None] * inv_freq[None, :]
    cos, sin = jnp.cos(ang), jnp.sin(ang)
    cos_t = jnp.full((seq, LANES), fill, F32).at[:, offset:offset + d].set(jnp.concatenate([cos, cos], -1))
    sin_t = jnp.zeros((seq, LANES), F32).at[:, offset:offset + d].set(jnp.concatenate([sin, sin], -1))
    return cos_t * scale, sin_t * scale


def _with_ones_lane(v):
    lane = lax.broadcasted_iota(jnp.int32, v.shape, 1)
    return jnp.where(lane % LANES == LANES - 1, 1.0, v)


def _rope_store(o_ref, h, hr, cos, sin):
    for s in range(o_ref.shape[1] // LANES):
        sl = slice(s * LANES, (s + 1) * LANES)
        o_ref[:, sl] = (h[:, sl] * cos + hr[:, sl] * sin).astype(o_ref.dtype)


def _proj_qkv_kernel(x_ref, wq_ref, wqr_ref, wk_ref, wkr_ref, wv_ref, cq_ref, sq_ref, ck_ref, sk_ref,
                     q_ref, k_ref, v_ref):
    xb = x_ref[...].astype(BF)
    _rope_store(q_ref, _dot(xb, wq_ref[...]), _dot(xb, wqr_ref[...]), cq_ref[...], sq_ref[...])
    _rope_store(k_ref, _dot(xb, wk_ref[...]), _dot(xb, wkr_ref[...]), ck_ref[...], sk_ref[...])
    v_ref[...] = _with_ones_lane(_dot(xb, wv_ref[...])).astype(v_ref.dtype)


def _proj_qkv(x2d, seq, wq, wqr, wk, wkr, wv, cq, sq, ck, sk):
    t = x2d.shape[0]
    tm = TM_PROJ
    nq, nv = wq.shape[1], wv.shape[1]
    pos_blocks = seq // tm
    full = lambda a: pl.BlockSpec(a.shape, lambda i: (0, 0))
    tab = pl.BlockSpec((tm, LANES), lambda i: (i % pos_blocks, 0))
    return pl.pallas_call(
        _proj_qkv_kernel,
        grid=(t // tm,),
        in_specs=[pl.BlockSpec((tm, D_MODEL), lambda i: (i, 0)),
                  full(wq), full(wqr), full(wk), full(wkr), full(wv), tab, tab, tab, tab],
        out_specs=[pl.BlockSpec((tm, nq), lambda i: (i, 0)),
                   pl.BlockSpec((tm, nq), lambda i: (i, 0)),
                   pl.BlockSpec((tm, nv), lambda i: (i, 0))],
        out_shape=[jax.ShapeDtypeStruct((t, nq), BF), jax.ShapeDtypeStruct((t, nq), BF),
                   jax.ShapeDtypeStruct((t, nv), BF)],
        compiler_params=_cparams(("parallel",)),
        name="proj_qkv",
    )(x2d, wq, wqr, wk, wkr, wv, cq, sq, ck, sk)


def _proj_dil_kernel(x_ref, wq_ref, wqr_ref, wk_ref, wkr_ref, wv_ref, cq_ref, sq_ref, ck_ref, sk_ref,
                     *refs):
    outs, scr = refs[:-1], refs[-1]
    n_slabs = scr.shape[0]
    xb = x_ref[...].astype(BF)

    def rope(h, hr, cos, sin):
        return jnp.concatenate(
            [h[:, s * LANES:(s + 1) * LANES] * cos + hr[:, s * LANES:(s + 1) * LANES] * sin
             for s in range(n_slabs)], axis=-1)

    def emit(h, which):
        for s in range(n_slabs):
            scr[s] = h[:, s * LANES:(s + 1) * LANES]
        for p in range(len(outs) // 3):
            o_ref = outs[3 * p + which]
            dil, rows, _ = o_ref.shape
            for c in range(dil):
                for s in range(n_slabs):
                    o_ref[c, :, s * LANES:(s + 1) * LANES] = (
                        scr[s, pl.ds(c, rows, stride=dil), :].astype(o_ref.dtype))

    emit(rope(_dot(xb, wq_ref[...]), _dot(xb, wqr_ref[...]), cq_ref[...], sq_ref[...]), 0)
    emit(rope(_dot(xb, wk_ref[...]), _dot(xb, wkr_ref[...]), ck_ref[...], sk_ref[...]), 1)
    emit(_dot(xb, wv_ref[...]), 2)


def _proj_dil(x2d, batch, seq, wq, wqr, wk, wkr, wv, cq, sq, ck, sk):
    t = x2d.shape[0]
    tm = TM_PROJ
    w = wq.shape[1]
    pos_blocks = seq // tm
    full = lambda a: pl.BlockSpec(a.shape, lambda i: (0, 0))
    tab = pl.BlockSpec((tm, LANES), lambda i: (i % pos_blocks, 0))
    out_specs, out_shape = [], []
    for _, dil in DIL_PAIRS:
        for _ in range(3):
            out_specs.append(pl.BlockSpec((None, dil, tm // dil, w),
                                          lambda i: (i // pos_blocks, 0, i % pos_blocks, 0)))
            out_shape.append(jax.ShapeDtypeStruct((batch, dil, seq // dil, w), BF))
    outs = pl.pallas_call(
        _proj_dil_kernel,
        grid=(t // tm,),
        in_specs=[pl.BlockSpec((tm, D_MODEL), lambda i: (i, 0)),
                  full(wq), full(wqr), full(wk), full(wkr), full(wv), tab, tab, tab, tab],
        out_specs=out_specs,
        out_shape=out_shape,
        scratch_shapes=[pltpu.VMEM((w // LANES, tm, LANES), F32)],
        compiler_params=_cparams(("parallel",)),
        name="proj_dil",
    )(x2d, wq, wqr, wk, wkr, wv, cq, sq, ck, sk)
    return [outs[3 * p:3 * p + 3] for p in range(len(DIL_PAIRS))]


def _rms(x, g, n):
    return x * lax.rsqrt(jnp.sum(x * x, axis=-1, keepdims=True) / n + LN_EPS) * g


def _proj_mla_kernel(x_ref, wcq_ref, wckv_ref, wkr_ref, wkrr_ref, gq_ref, gkv_ref,
                     wuq_ref, wuqr_ref, wuk_ref, wuv_ref, cq_ref, sq_ref, ck_ref, sk_ref,
                     q_ref, k_ref, v_ref):
    xb = x_ref[...].astype(BF)
    nq = _rms(_dot(xb, wcq_ref[...]), gq_ref[...], MLA_Q_RANK).astype(BF)
    nkv = _rms(_dot(xb, wckv_ref[...]), gkv_ref[...], MLA_KV_RANK).astype(BF)
    _rope_store(q_ref, _dot(nq, wuq_ref[...]), _dot(nq, wuqr_ref[...]), cq_ref[...], sq_ref[...])
    kpe = _dot(xb, wkr_ref[...]) * ck_ref[...] + _dot(xb, wkrr_ref[...]) * sk_ref[...]
    kn = _dot(nkv, wuk_ref[...])
    for s in range(MLA_HEADS):
        sl = slice(s * LANES, (s + 1) * LANES)
        k_ref[:, sl] = (kn[:, sl] + kpe).astype(k_ref.dtype)
    v_ref[...] = _with_ones_lane(_dot(nkv, wuv_ref[...])).astype(v_ref.dtype)


def _proj_mla(x2d, seq, wcq, wckv, wkr, wkrr, gq, gkv, wuq, wuqr, wuk, wuv, cq, sq, ck, sk):
    t = x2d.shape[0]
    tm = TM_PROJ
    n = MLA_HEADS * LANES
    pos_blocks = seq // tm
    full = lambda a: pl.BlockSpec(a.shape, lambda i: (0, 0))
    tab = pl.BlockSpec((tm, LANES), lambda i: (i % pos_blocks, 0))
    out = pl.BlockSpec((tm, n), lambda i: (i, 0))
    return pl.pallas_call(
        _proj_mla_kernel,
        grid=(t // tm,),
        in_specs=[pl.BlockSpec((tm, D_MODEL), lambda i: (i, 0)),
                  full(wcq), full(wckv), full(wkr), full(wkrr), full(gq), full(gkv),
                  full(wuq), full(wuqr), full(wuk), full(wuv), tab, tab, tab, tab],
        out_specs=[out, out, out],
        out_shape=[jax.ShapeDtypeStruct((t, n), BF)] * 3,
        compiler_params=_cparams(("parallel",)),
        name="proj_mla",
    )(x2d, wcq, wckv, wkr, wkrr, gq, gkv, wuq, wuqr, wuk, wuv, cq, sq, ck, sk)


def _softmax_pv(q, k, v):
    s = _dot_nt(q, k)
    m = jnp.max(s, axis=-1, keepdims=True)
    p = jnp.exp2(s - m)
    o = _dot(p.astype(BF), v)
    return o / o[:, LANES - 1:LANES]


def _diff_attn_kernel(lam_ref, g_ref, q1_ref, q2_ref, k1_ref, k2_ref, v_ref, o_ref, *, lambda_init):
    lam = lam_ref[...]
    lam_full = (jnp.exp(jnp.sum(lam[0:1] * lam[1:2], axis=-1, keepdims=True))
                - jnp.exp(jnp.sum(lam[2:3] * lam[3:4], axis=-1, keepdims=True)) + lambda_init)
    v = v_ref[0]
    o1 = _softmax_pv(q1_ref[0], k1_ref[0], v)
    o2 = _softmax_pv(q2_ref[0], k2_ref[0], v)
    lane = lax.broadcasted_iota(jnp.int32, o1.shape, 1)
    o = jnp.where(lane == LANES - 1, 0.0, o1 - lam_full * o2)
    o_ref[0] = (_rms(o, g_ref[...], DIFF_V_DIM) * (1.0 - lambda_init)).astype(o_ref.dtype)


def _diff_attn(q, k, v, lam, g_pad, lambda_init):
    b, s, _ = q.shape
    tq = TQ_DENSE
    qspec = lambda off: pl.BlockSpec((1, tq, LANES), lambda bi, h, i: (bi, i, 2 * h + off))
    kspec = lambda off: pl.BlockSpec((1, s, LANES), lambda bi, h, i: (bi, 0, 2 * h + off))
    return pl.pallas_call(
        functools.partial(_diff_attn_kernel, lambda_init=lambda_init),
        grid=(b, DIFF_HEADS, s // tq),
        in_specs=[pl.BlockSpec(lam.shape, lambda bi, h, i: (0, 0)),
                  pl.BlockSpec(g_pad.shape, lambda bi, h, i: (0, 0)),
                  qspec(0), qspec(1), kspec(0), kspec(1),
                  pl.BlockSpec((1, s, LANES), lambda bi, h, i: (bi, 0, h))],
        out_specs=pl.BlockSpec((1, tq, LANES), lambda bi, h, i: (bi, i, h)),
        out_shape=jax.ShapeDtypeStruct((b, s, DIFF_HEADS * LANES), BF),
        compiler_params=_cparams(("parallel", "parallel", "arbitrary")),
        name="diff_attn",
    )(lam, g_pad, q, q, k, k, v)


def _mla_attn_kernel(q_ref, k_ref, v_ref, o_ref):
    o_ref[0] = _softmax_pv(q_ref[0], k_ref[0], v_ref[0]).astype(o_ref.dtype)


def _mla_attn(q, k, v):
    b, s, _ = q.shape
    tq = TQ_DENSE
    kv = pl.BlockSpec((1, s, LANES), lambda bi, h, i: (bi, 0, h))
    qo = pl.BlockSpec((1, tq, LANES), lambda bi, h, i: (bi, i, h))
    return pl.pallas_call(
        _mla_attn_kernel,
        grid=(b, MLA_HEADS, s // tq),
        in_specs=[qo, kv, kv],
        out_specs=qo,
        out_shape=jax.ShapeDtypeStruct((b, s, MLA_HEADS * LANES), BF),
        compiler_params=_cparams(("parallel", "parallel", "arbitrary")),
        name="mla_attn",
    )(q, k, v)


def _band_attn_kernel(q_ref, kp_ref, kc_ref, kn_ref, vp_ref, vc_ref, vn_ref, o_ref, lse_ref, *, seg_len):
    i = pl.program_id(2)
    tq = q_ref.shape[0]
    blk = LANES
    span = blk + 2 * BAND_HALF
    kk = jnp.concatenate([kp_ref[...], kc_ref[...], kn_ref[...]], axis=0)
    vv = jnp.concatenate([vp_ref[...], vc_ref[...], vn_ref[...]], axis=0)
    qi = lax.broadcasted_iota(jnp.int32, (blk, span), 0)
    kj = lax.broadcasted_iota(jnp.int32, (blk, span), 1)
    for j in range(tq // blk):
        qpos = i * tq + j * blk + qi
        kpos = i * tq + j * blk - BAND_HALF + kj
        mask = (jnp.abs(kpos - qpos) <= BAND_HALF) & (kpos >= 0) & (kpos < seg_len)
        for h in range(DIL_HEADS):
            cs = slice(h * LANES, (h + 1) * LANES)
            rows = slice(j * blk, (j + 1) * blk)
            k = kk[j * blk:j * blk + span, cs]
            v = vv[j * blk:j * blk + span, cs]
            s = jnp.where(mask, _dot_nt(q_ref[rows, cs], k), NEG_INF)
            m = jnp.max(s, axis=-1, keepdims=True)
            p = jnp.exp2(s - m)
            l = jnp.sum(p, axis=-1, keepdims=True)
            o_ref[rows, cs] = _dot(p.astype(BF), v) / l
            lse_ref[rows, cs] = jnp.broadcast_to(m * LN2 + jnp.log(l), (blk, LANES))


def _band_attn(q, k, v):
    b, dil, seg, w = q.shape
    tq = TQ_BAND
    r = tq // BAND_HALF
    nb = seg // BAND_HALF
    cur = pl.BlockSpec((None, None, tq, w), lambda bi, c, i: (bi, c, i, 0))
    prev = pl.BlockSpec((None, None, BAND_HALF, w), lambda bi, c, i: (bi, c, jnp.maximum(i * r - 1, 0), 0))
    nxt = pl.BlockSpec((None, None, BAND_HALF, w),
                       lambda bi, c, i: (bi, c, jnp.minimum((i + 1) * r, nb - 1), 0))
    return pl.pallas_call(
        functools.partial(_band_attn_kernel, seg_len=seg),
        grid=(b, dil, seg // tq),
        in_specs=[cur, prev, cur, nxt, prev, cur, nxt],
        out_specs=[cur, cur],
        out_shape=[jax.ShapeDtypeStruct((b, dil, seg, w), F32)] * 2,
        compiler_params=_cparams(("parallel", "parallel", "arbitrary")),
        name="band_attn",
    )(q, k, k, k, v, v, v)


def _layer_norm(y, g, b):
    mu = jnp.mean(y, axis=-1, keepdims=True)
    yc = y - mu
    var = jnp.mean(yc * yc, axis=-1, keepdims=True)
    return yc * lax.rsqrt(var + LN_EPS) * g + b


def _to_token_order(src_ref, dst_ref):
    dil, rows, w = src_ref.shape
    for c in range(dil):
        for s in range(w // LANES):
            dst_ref[s, pl.ds(c, rows, stride=dil), :] = src_ref[c, :, s * LANES:(s + 1) * LANES]


def _out_proj_kernel(x_ref, oa_ref, ob_ref, o1_ref, o2_ref, o3_ref, l1_ref, l2_ref, l3_ref,
                     wa_ref, wb_ref, wc_ref, g_ref, b_ref, y_ref, o2_scr, o3_scr, l2_scr, l3_scr):
    _to_token_order(o2_ref, o2_scr)
    _to_token_order(o3_ref, o3_scr)
    _to_token_order(l2_ref, l2_scr)
    _to_token_order(l3_ref, l3_scr)
    heads = []
    for s in range(DIL_HEADS):
        sl = slice(s * LANES, (s + 1) * LANES)
        l1, l2, l3 = l1_ref[0, :, sl], l2_scr[s], l3_scr[s]
        m = jnp.maximum(jnp.maximum(l1, l2), l3)
        e1, e2, e3 = jnp.exp(l1 - m), jnp.exp(l2 - m), jnp.exp(l3 - m)
        oc = (e1 * o1_ref[0, :, sl] + e2 * o2_scr[s] + e3 * o3_scr[s]) / (e1 + e2 + e3)
        heads.append(oc.astype(BF))
    mix = (_dot(oa_ref[...], wa_ref[...]) + _dot(ob_ref[...], wb_ref[...])
           + _dot(jnp.concatenate(heads, axis=-1), wc_ref[...]))
    y_ref[...] = _layer_norm(DEEPNORM_ALPHA * x_ref[...] + mix, g_ref[...], b_ref[...])


def _out_proj(x2d, seq, oa, ob, band, wa, wb, wc, g, bb):
    t = x2d.shape[0]
    tm = TM_PROJ
    pos_blocks = seq // tm
    w = DIL_HEADS * LANES
    row = lambda a: pl.BlockSpec((tm, a.shape[1]), lambda i: (i, 0))
    full = lambda a: pl.BlockSpec(a.shape, lambda i: (0, 0))
    cls = lambda a: pl.BlockSpec((None, a.shape[1], tm // a.shape[1], w),
                                 lambda i: (i // pos_blocks, 0, i % pos_blocks, 0))
    (o1, l1), (o2, l2), (o3, l3) = band
    slab = pltpu.VMEM((DIL_HEADS, tm, LANES), F32)
    return pl.pallas_call(
        _out_proj_kernel,
        grid=(t // tm,),
        in_specs=[row(x2d), row(oa), row(ob), cls(o1), cls(o2), cls(o3), cls(l1), cls(l2), cls(l3),
                  full(wa), full(wb), full(wc), full(g), full(bb)],
        out_specs=row(x2d),
        out_shape=jax.ShapeDtypeStruct((t, D_MODEL), F32),
        scratch_shapes=[slab, slab, slab, slab],
        compiler_params=_cparams(("parallel",)),
        name="out_proj_ln",
    )(x2d, oa, ob, o1, o2, o3, l1, l2, l3, wa, wb, wc, g, bb)


def _sort16_network():
    pairs, p = [], 1
    while p < PEER_TOPK:
        k = p
        while k >= 1:
            for j in range(k % p, PEER_TOPK - k, 2 * k):
                for i in range(min(k, PEER_TOPK - j - k)):
                    if (i + j) // (2 * p) == (i + j + k) // (2 * p):
                        pairs.append((i + j, i + j + k))
            k //= 2
        p *= 2
    return pairs


def _compare_swap(v, i, j):
    v[i], v[j] = jnp.maximum(v[i], v[j]), jnp.minimum(v[i], v[j])


def _top16_sorted(s):
    v = [s[i * SUBLANES:(i + 1) * SUBLANES] for i in range(PEER_TOPK)]
    for i, j in _sort16_network():
        _compare_swap(v, i, j)
    shift = SUBLANES // 2
    while shift >= 1:
        other = [pltpu.roll(x, shift, 0) for x in v]
        v = [jnp.maximum(v[i], other[PEER_TOPK - 1 - i]) for i in range(PEER_TOPK)]
        d = PEER_TOPK // 2
        while d >= 1:
            for i in range(PEER_TOPK):
                if i & d == 0:
                    _compare_swap(v, i, i + d)
            d //= 2
        shift //= 2
    return v


def _route_chunk(s1, s2):
    v1 = _top16_sorted(s1)
    v2 = _top16_sorted(s2)
    t1 = jnp.concatenate([x[0:1] for x in v1], axis=0)
    t2 = jnp.concatenate([x[0:1] for x in v2], axis=0)
    cand = jnp.concatenate(
        [t1[0:1] + t2]
        + [t1[k1:k1 + 1] + t2[0:8] for k1 in range(1, 8)]
        + [t1[8:16] + t2[0:1]], axis=0)
    c0 = t1[0:1] + t2[0:1]
    work, z, tau = cand, jnp.zeros_like(c0), c0
    for _ in range(PEER_TOPK):
        tau = jnp.max(work, axis=0, keepdims=True)
        work = jnp.where(work == tau, -jnp.inf, work)
        z = z + jnp.exp(tau - c0)
    cnt_of_rank = jnp.zeros_like(t1)
    for k2 in range(PEER_TOPK):
        cnt_of_rank = cnt_of_rank + jnp.where(t1 + t2[k2:k2 + 1] >= tau, 1.0, 0.0)
    e2_top = jnp.exp(t2 - t2[0:1])
    thr_of_rank = jnp.full_like(t1, NO_SECOND_KEY)
    for c in range(1, PEER_TOPK + 1):
        thr_of_rank = jnp.where(cnt_of_rank == float(c), e2_top[c - 1:c], thr_of_rank)
    thr = [jnp.full((SUBLANES, s1.shape[1]), NO_SECOND_KEY, F32)] * PEER_TOPK
    for k1 in range(PEER_TOPK):
        row = jnp.broadcast_to(thr_of_rank[k1:k1 + 1], (SUBLANES, s1.shape[1]))
        thr = [jnp.where(s1[i * SUBLANES:(i + 1) * SUBLANES] == v1[k1], row, thr[i])
               for i in range(PEER_TOPK)]
    e1 = jnp.exp(s1 - t1[0:1]) / z
    e2 = jnp.exp(s2 - t2[0:1])
    return e1, jnp.concatenate(thr, axis=0), e2


def _peer_route_kernel(x_ref, wq_ref, keys_ref, e1_ref, thr_ref, e2_ref, q_scr, s_scr):
    nk = PEER_N_KEYS
    n_chunks = e1_ref.shape[0]
    q = _dot(x_ref[...].astype(BF), wq_ref[...]).astype(BF)
    for s in range(2 * PEER_HEADS):
        q_scr[s] = q[:, s * nk:(s + 1) * nk]

    def head_body(h, carry):
        s1 = _dot_nt(keys_ref[2 * h], q_scr[2 * h])
        s2 = _dot_nt(keys_ref[2 * h + 1], q_scr[2 * h + 1])
        for c in range(n_chunks):
            s_scr[0, c] = s1[:, c * LANES:(c + 1) * LANES]
            s_scr[1, c] = s2[:, c * LANES:(c + 1) * LANES]
        rows = pl.ds(pl.multiple_of(h * nk, nk), nk)

        def chunk_body(c, carry2):
            e1, thr, e2 = _route_chunk(s_scr[0, c], s_scr[1, c])
            step_shape = (e1_ref.shape[1], e1_ref.shape[3], LANES)
            e1_ref[c, :, h] = (e1 * SQRT_HALF).reshape(step_shape)
            thr_ref[c, :, h] = thr.reshape(step_shape)
            e2_ref[c, rows, :] = e2
            return carry2

        lax.fori_loop(0, n_chunks, chunk_body, 0)
        return carry

    lax.fori_loop(0, PEER_HEADS, head_body, 0)


def _peer_route(x2d, wq, keys):
    t = x2d.shape[0]
    tt = TT_PEER
    nc = tt // LANES
    rows = PEER_HEADS * PEER_N_KEYS
    out = pl.BlockSpec((nc, rows, LANES), lambda i: (i, 0, 0))
    shape = jax.ShapeDtypeStruct((t // LANES, rows, LANES), F32)
    rps = EB_PEER // PEER_N_KEYS
    first = (PEER_N_KEYS // rps, PEER_HEADS, rps, LANES)
    out1 = pl.BlockSpec((nc,) + first, lambda i: (i, 0, 0, 0, 0))
    shape1 = jax.ShapeDtypeStruct((t // LANES,) + first, F32)
    return pl.pallas_call(
        _peer_route_kernel,
        grid=(t // tt,),
        in_specs=[pl.BlockSpec((tt, D_MODEL), lambda i: (i, 0)),
                  pl.BlockSpec(wq.shape, lambda i: (0, 0)),
                  pl.BlockSpec(keys.shape, lambda i: (0, 0, 0))],
        out_specs=[out1, out1, out],
        out_shape=[shape1, shape1, shape],
        scratch_shapes=[pltpu.VMEM((2 * PEER_HEADS, tt, PEER_N_KEYS), BF),
                        pltpu.VMEM((2, nc, PEER_N_KEYS, LANES), F32)],
        compiler_params=_cparams(("parallel",)),
        name="peer_route",
    )(x2d, wq, keys)


def _peer_expert_kernel(x_ref, e1_ref, thr_ref, e2_ref, u_ref, vt_ref, g_ref, b_ref,
                        y_ref, acc_ref, a_ref, xb_ref):
    j = pl.program_id(1)
    nk = PEER_N_KEYS
    rows_per_step = u_ref.shape[0] // nk
    n_chunks = e2_ref.shape[0]

    @pl.when(j == 0)
    def _():
        acc_ref[...] = jnp.zeros_like(acc_ref)
        xb_ref[...] = (x_ref[...] * SQRT_HALF).astype(BF)

    n_tiles = nk // SUBLANES
    ht = _dot_nt(u_ref[...], xb_ref[...])
    for c in range(n_chunks):
        cs = slice(c * LANES, (c + 1) * LANES)
        for r0 in range(0, rows_per_step, ROW_GROUP):
            w = [[jnp.zeros((SUBLANES, LANES), F32)] * n_tiles for _ in range(ROW_GROUP)]
            for h in range(PEER_HEADS):
                e1 = [jnp.broadcast_to(e1_ref[c, h, r0 + r:r0 + r + 1, :], (SUBLANES, LANES))
                      for r in range(ROW_GROUP)]
                thr = [jnp.broadcast_to(thr_ref[c, h, r0 + r:r0 + r + 1, :], (SUBLANES, LANES))
                       for r in range(ROW_GROUP)]
                for k in range(n_tiles):
                    first = (h * n_tiles + k) * SUBLANES
                    e2 = e2_ref[c, first:first + SUBLANES, :]
                    for r in range(ROW_GROUP):
                        w[r][k] = w[r][k] + jnp.where(e2 >= thr[r], e1[r] * e2, 0.0)
            for r in range(ROW_GROUP):
                for k in range(0, n_tiles, 2):
                    bs = slice((r0 + r) * nk + k * SUBLANES, (r0 + r) * nk + (k + 2) * SUBLANES)
                    gate = jnp.concatenate([w[r][k], w[r][k + 1]], axis=0)
                    hb = ht[bs, cs]
                    a_ref[bs, cs] = (gate * (hb * (1.0 + lax.erf(hb)))).astype(BF)
    acc_ref[...] += _dot(vt_ref[...], a_ref[...])

    @pl.when(j == pl.num_programs(1) - 1)
    def _():
        ffn = acc_ref[...].T
        y_ref[...] = _layer_norm(DEEPNORM_ALPHA * x_ref[...] + ffn, g_ref[...], b_ref[...])


def _peer_expert(x2d, e1, thr, e2, u, vt, g, bb):
    t = x2d.shape[0]
    tt, eb = TT_PEER, EB_PEER
    nc = tt // LANES
    rows = PEER_HEADS * PEER_N_KEYS
    route = pl.BlockSpec((nc, rows, LANES), lambda i, j: (i, 0, 0))
    first = pl.BlockSpec((nc, None, PEER_HEADS, eb // PEER_N_KEYS, LANES), lambda i, j: (i, j, 0, 0, 0))
    vec = pl.BlockSpec((1, D_MODEL), lambda i, j: (0, 0))
    return pl.pallas_call(
        _peer_expert_kernel,
        grid=(t // tt, PEER_N_EXPERTS // eb),
        in_specs=[pl.BlockSpec((tt, D_MODEL), lambda i, j: (i, 0)),
                  first, first, route,
                  pl.BlockSpec((eb, D_MODEL), lambda i, j: (j, 0)),
                  pl.BlockSpec((D_MODEL, eb), lambda i, j: (0, j)),
                  vec, vec],
        out_specs=pl.BlockSpec((tt, D_MODEL), lambda i, j: (i, 0)),
        out_shape=jax.ShapeDtypeStruct((t, D_MODEL), F32),
        scratch_shapes=[pltpu.VMEM((D_MODEL, tt), F32), pltpu.VMEM((eb, tt), BF),
                        pltpu.VMEM((tt, D_MODEL), BF)],
        compiler_params=_cparams(("parallel", "arbitrary")),
        name="peer_expert_ln",
    )(x2d, e1, thr, e2, u, vt, g, bb)


def _split_w_in(w):
    sizes = (256, 256, 256, MLA_Q_RANK, MLA_KV_RANK, MLA_ROPE_DIM, 384, 384, 384)
    out, off = [], 0
    for n in sizes:
        out.append(w[:, off:off + n])
        off += n
    return out


def _pad_rows(w, n_heads, d):
    n = w.shape[1]
    w3 = w.reshape(n_heads, d, n)
    return jnp.zeros((n_heads, LANES, n), w.dtype).at[:, :d, :].set(w3).reshape(n_heads * LANES, n)


def _layer(x2d, batch, seq, l, w_in, diff_lambda, diff_subln_g, mla_q_norm_g, mla_kv_norm_g, mla_w_uq,
           mla_w_ukv, w_o, ln1_g, ln1_b, peer_w_q, peer_sub_keys, peer_u, peer_v, ln2_g, ln2_b):
    lambda_init = 0.8 - 0.6 * math.exp(-0.3 * (l + 1))
    a_q, a_k, a_v, b_cq, b_ckv, b_kr, c_q, c_k, c_v = _split_w_in(w_in)
    bf = lambda a: a.astype(BF)

    cq, sq = _rope_tables(seq, DIFF_QK_DIM, 0, LOG2E * DIFF_QK_DIM ** -0.5, 0.0)
    ck, sk = _rope_tables(seq, DIFF_QK_DIM, 0, 1.0, 0.0)
    nh = 2 * DIFF_HEADS
    qa, ka, va = _proj_qkv(
        x2d, seq,
        bf(_pad_heads(a_q, nh, DIFF_QK_DIM)), bf(_pad_heads(_rot_cols(a_q, nh, DIFF_QK_DIM), nh, DIFF_QK_DIM)),
        bf(_pad_heads(a_k, nh, DIFF_QK_DIM)), bf(_pad_heads(_rot_cols(a_k, nh, DIFF_QK_DIM), nh, DIFF_QK_DIM)),
        bf(_pad_heads(a_v, DIFF_HEADS, DIFF_V_DIM)), cq, sq, ck, sk)
    g_pad = jnp.zeros((1, LANES), F32).at[0, :DIFF_V_DIM].set(diff_subln_g.astype(F32))
    sh = lambda a: a.reshape(batch, seq, a.shape[-1])
    out_a = _diff_attn(sh(qa), sh(ka), sh(va), diff_lambda.astype(F32), g_pad, lambda_init)

    qk_dim = MLA_NOPE_DIM + MLA_ROPE_DIM
    cqm, sqm = _rope_tables(seq, MLA_ROPE_DIM, MLA_NOPE_DIM, LOG2E * qk_dim ** -0.5, 1.0)
    cqm = cqm.at[:, qk_dim:].set(0.0)
    ckm, skm = _rope_tables(seq, MLA_ROPE_DIM, MLA_NOPE_DIM, 1.0, 0.0)
    uq3 = mla_w_uq.reshape(MLA_Q_RANK, MLA_HEADS, qk_dim)
    uq_rope = uq3[..., MLA_NOPE_DIM:].reshape(MLA_Q_RANK, MLA_HEADS * MLA_ROPE_DIM)
    uq_rot = _pad_heads(_rot_cols(uq_rope, MLA_HEADS, MLA_ROPE_DIM), MLA_HEADS, MLA_ROPE_DIM, MLA_NOPE_DIM)
    ukv3 = mla_w_ukv.reshape(MLA_KV_RANK, MLA_HEADS, MLA_NOPE_DIM + MLA_V_DIM)
    uk = ukv3[..., :MLA_NOPE_DIM].reshape(MLA_KV_RANK, MLA_HEADS * MLA_NOPE_DIM)
    uv = ukv3[..., MLA_NOPE_DIM:].reshape(MLA_KV_RANK, MLA_HEADS * MLA_V_DIM)
    qb, kb, vb = _proj_mla(
        x2d, seq, bf(b_cq), bf(b_ckv),
        bf(_pad_heads(b_kr, 1, MLA_ROPE_DIM, MLA_NOPE_DIM)),
        bf(_pad_heads(_rot_cols(b_kr, 1, MLA_ROPE_DIM), 1, MLA_ROPE_DIM, MLA_NOPE_DIM)),
        mla_q_norm_g.astype(F32)[None, :], mla_kv_norm_g.astype(F32)[None, :],
        bf(_pad_heads(mla_w_uq, MLA_HEADS, qk_dim)), bf(uq_rot),
        bf(_pad_heads(uk, MLA_HEADS, MLA_NOPE_DIM)), bf(_pad_heads(uv, MLA_HEADS, MLA_V_DIM)),
        cqm, sqm, ckm, skm)
    out_b = _mla_attn(sh(qb), sh(kb), sh(vb))

    cqd, sqd = _rope_tables(seq, DIL_HEAD_DIM, 0, LOG2E * DIL_HEAD_DIM ** -0.5, 0.0)
    ckd, skd = _rope_tables(seq, DIL_HEAD_DIM, 0, 1.0, 0.0)
    nh = DIL_HEADS
    dil_qkv = _proj_dil(
        x2d, batch, seq,
        bf(_pad_heads(c_q, nh, DIL_HEAD_DIM)), bf(_pad_heads(_rot_cols(c_q, nh, DIL_HEAD_DIM), nh, DIL_HEAD_DIM)),
        bf(_pad_heads(c_k, nh, DIL_HEAD_DIM)), bf(_pad_heads(_rot_cols(c_k, nh, DIL_HEAD_DIM), nh, DIL_HEAD_DIM)),
        bf(_pad_heads(c_v, nh, DIL_HEAD_DIM)), cqd, sqd, ckd, skd)
    band = [_band_attn(q, k, v) for q, k, v in dil_qkv]
    fl = lambda a: a.reshape(batch * seq, a.shape[-1])

    wo_a = _pad_rows(w_o[:DIFF_HEADS * DIFF_V_DIM], DIFF_HEADS, DIFF_V_DIM)
    wo_b = _pad_rows(w_o[256:256 + MLA_HEADS * MLA_V_DIM], MLA_HEADS, MLA_V_DIM)
    wo_c = _pad_rows(w_o[640:], DIL_HEADS, DIL_HEAD_DIM)
    x1 = _out_proj(x2d, seq, fl(out_a), fl(out_b), band,
                   bf(wo_a), bf(wo_b), bf(wo_c), ln1_g.astype(F32)[None, :], ln1_b.astype(F32)[None, :])

    keys = bf(peer_sub_keys.reshape(PEER_HEADS * 2, PEER_N_KEYS, PEER_HALF_DIM))
    e1, thr, e2 = _peer_route(x1, bf(peer_w_q), keys)
    return _peer_expert(x1, e1, thr, e2, bf(peer_u), bf(peer_v).T,
                        ln2_g.astype(F32)[None, :], ln2_b.astype(F32)[None, :])


def kernel(x, w_in, diff_lambda, diff_subln_g, mla_q_norm_g, mla_kv_norm_g, mla_w_uq, mla_w_ukv, w_o,
           ln1_g, ln1_b, peer_w_q, peer_sub_keys, peer_u, peer_v, ln2_g, ln2_b):
    batch, seq, d = x.shape
    params = (w_in, diff_lambda, diff_subln_g, mla_q_norm_g, mla_kv_norm_g, mla_w_uq, mla_w_ukv, w_o,
              ln1_g, ln1_b, peer_w_q, peer_sub_keys, peer_u, peer_v, ln2_g, ln2_b)
    x2d = x.reshape(batch * seq, d)
    for l in range(DEPTH):
        x2d = _layer(x2d, batch, seq, l, *[p[l] for p in params])
    return x2d.reshape(batch, seq, d)
```
